```python
import functools
import jax
import jax.numpy as jnp
from jax import lax
import numpy as np

D_MODEL = 1024
BATCH = 4
SEQ = 8192
DEPTH = 2
DEC_BATCH = 32
DEC_SEQ = 4
PAST_LEN = 16384
PAGE_SIZE = 128

HEAD_DIM = 64
D_MIX = D_MODEL
D_RWKV = D_MIX // 2
D_MOBA = D_MIX - D_RWKV
H_RWKV = D_RWKV // HEAD_DIM
H_MOBA = D_MOBA // HEAD_DIM
DECAY_LORA = 64
ICLR_LORA = 64
GATE_LORA = 128
C_RWKV_IN = 3 * D_RWKV + DECAY_LORA + ICLR_LORA + GATE_LORA
C_MOBA_IN = 3 * D_MOBA
C_IN = C_RWKV_IN + C_MOBA_IN
D_FF = 2816
MOBA_BLOCK = 256
MOBA_TOPK = 3
Q_CHUNK = 32
ROPE_THETA = 10000.0
NORM_EPS = 1e-6
GN_EPS = 64e-5
NEG_INF = -1e30
F32 = jnp.float32

kernel_name = 'hymba_rwkv7_moba_macaron_step'


def rms_norm(x, g):
    x32 = x.astype(F32)
    y = x32 * lax.rsqrt(jnp.mean(x32 * x32, axis=-1, keepdims=True) + NORM_EPS)
    return (y * g.astype(F32)).astype(x.dtype)


def ffn_half(x, g, w_gate, w_up, w_down):
    h = rms_norm(x, g)
    return x + 0.5 * ((jax.nn.silu(h @ w_gate) * (h @ w_up)) @ w_down)


def rope(x, pos):
    half = HEAD_DIM // 2
    inv_freq = ROPE_THETA ** (-jnp.arange(half, dtype=F32) / half)
    ang = pos.astype(F32)[:, None] * inv_freq[None, :]
    cos = jnp.cos(ang)[None, :, None, :]
    sin = jnp.sin(ang)[None, :, None, :]
    x32 = x.astype(F32)
    x1, x2 = x32[..., :half], x32[..., half:]
    return jnp.concatenate([x1 * cos - x2 * sin, x2 * cos + x1 * sin], axis=-1).astype(x.dtype)


def rwkv_inputs(p, p_prev, lp):
    B, T, _ = p.shape
    f = lambda n: lp[n].astype(F32)
    p = p.astype(F32)
    p_shift = jnp.concatenate([p_prev.astype(F32)[:, None, :], p[:, :-1]], axis=1)
    xm = p + (p_shift - p) * f('mu')
    cuts = (D_RWKV, 2 * D_RWKV, 3 * D_RWKV, 3 * D_RWKV + DECAY_LORA, 3 * D_RWKV + DECAY_LORA + ICLR_LORA)
    r, k, v, xw, xa, xg = jnp.split(xm, cuts, axis=-1)
    w_log = -jax.nn.softplus(-(f('w0') + jnp.tanh(xw) @ f('w2'))) - 0.5
    decay = jnp.exp(-jnp.exp(w_log))
    a = jax.nn.sigmoid(f('a0') + xa @ f('a2'))
    g = jax.nn.sigmoid(xg) @ f('g2')
    heads = lambda t: t.reshape(B, T, H_RWKV, HEAD_DIM)
    kk = heads(k * f('k_k'))
    kk = kk / jnp.maximum(jnp.sqrt(jnp.sum(kk * kk, axis=-1, keepdims=True)), 1e-12)
    k = k * (1.0 + (a - 1.0) * f('k_a'))
    return heads(r), heads(decay), heads(k), heads(v), kk, heads(a), g


def wkv_scan(r, decay, k, v, kk, a, s0):
    def step(S, inp):
        r_t, w_t, k_t, v_t, kk_t, a_t = inp
        sa = jnp.einsum('bhvk,bhk->bhv', S, -kk_t)
        S = S * w_t[:, :, None, :] + sa[..., None] * (kk_t * a_t)[:, :, None, :] + v_t[..., None] * k_t[:, :, None, :]
        return S, jnp.einsum('bhvk,bhk->bhv', S, r_t)
    xs = tuple(jnp.moveaxis(t, 1, 0) for t in (r, decay, k, v, kk, a))
    s_fin, y = lax.scan(step, s0, xs)
    return jnp.moveaxis(y, 0, 1), s_fin


def rwkv_output(y, r, k, v, g, lp):
    B, T = y.shape[:2]
    mean = jnp.mean(y, axis=-1, keepdims=True)
    var = jnp.mean(jnp.square(y - mean), axis=-1, keepdims=True)
    ln_w = lp['ln_w'].astype(F32).reshape(H_RWKV, HEAD_DIM)
    ln_b = lp['ln_b'].astype(F32).reshape(H_RWKV, HEAD_DIM)
    yn = (y - mean) * lax.rsqrt(var + GN_EPS) * ln_w + ln_b
    bonus = jnp.sum(r * k * lp['r_k'].astype(F32), axis=-1, keepdims=True) * v
    return (yn + bonus).reshape(B, T, D_RWKV) * g


def moba_prompt(q, k, v):
    B, S, H, D = q.shape
    nb = -(-S // MOBA_BLOCK)
    pad = nb * MOBA_BLOCK - S
    blocks = lambda t: jnp.pad(t, ((0, 0), (0, pad), (0, 0), (0, 0))).reshape(B, nb, MOBA_BLOCK, H, D).transpose(0, 3, 1, 2, 4)
    kb, vb = blocks(k), blocks(v)
    kmean = jnp.mean(kb.astype(F32), axis=3)
    n_sel = min(MOBA_TOPK, nb - 1)
    scale = HEAD_DIM ** -0.5
    bi = jnp.arange(B)[:, None, None, None]
    hi = jnp.arange(H)[None, :, None, None]

    def chunk(ci):
        start = ci * Q_CHUNK
        own = start // MOBA_BLOCK
        qc = lax.dynamic_slice_in_dim(q, start, Q_CHUNK, axis=1).transpose(0, 2, 1, 3)
        qpos = start + jnp.arange(Q_CHUNK)
        k_own = lax.dynamic_index_in_dim(kb, own, axis=2, keepdims=False)
        v_own = lax.dynamic_index_in_dim(vb, own, axis=2, keepdims=False)
        kpos = own * MOBA_BLOCK + jnp.arange(MOBA_BLOCK)
        s_own = jnp.einsum('bhqd,bhkd->bhqk', qc, k_own, preferred_element_type=F32) * scale
        s_own = jnp.where(kpos[None, :] <= qpos[:, None], s_own, NEG_INF)
        if n_sel == 0:
            p = jax.nn.softmax(s_own, axis=-1).astype(v.dtype)
            return jnp.einsum('bhqk,bhkd->bqhd', p, v_own)
        gate = jnp.einsum('bhqd,bhnd->bhqn', qc.astype(F32), kmean)
        gate = jnp.where(jnp.arange(nb) < own, gate, NEG_INF)
        _, idx = lax.top_k(gate, n_sel)
        valid = idx < own
        k_sel = kb[bi, hi, idx]
        v_sel = vb[bi, hi, idx]
        s_sel = jnp.einsum('bhqd,bhqjkd->bhqjk', qc, k_sel, preferred_element_type=F32) * scale
        s_sel = jnp.where(valid[..., None], s_sel, NEG_INF).reshape(B, H, Q_CHUNK, n_sel * MOBA_BLOCK)
        p = jax.nn.softmax(jnp.concatenate([s_sel, s_own], axis=-1), axis=-1).astype(v.dtype)
        p_sel = p[..., :n_sel * MOBA_BLOCK].reshape(B, H, Q_CHUNK, n_sel, MOBA_BLOCK)
        p_own = p[..., n_sel * MOBA_BLOCK:]
        return jnp.einsum('bhqjk,bhqjkd->bqhd', p_sel, v_sel) + jnp.einsum('bhqk,bhkd->bqhd', p_own, v_own)

    outs = lax.map(chunk, jnp.arange(S // Q_CHUNK))
    return outs.transpose(1, 0, 2, 3, 4).reshape(B, S, H, D)


def moba_sample(cache_k, cache_v, page_table, q, k, v):
    DB, T, H, D = q.shape
    n_pages = page_table.shape[1]
    ppb = MOBA_BLOCK // PAGE_SIZE
    n_full = (n_pages * PAGE_SIZE) // MOBA_BLOCK
    own_page0 = n_full * ppb
    n_own_past = (n_pages - own_page0) * PAGE_SIZE
    n_sel = min(MOBA_TOPK, n_full)
    scale = HEAD_DIM ** -0.5
    qt = q.transpose(0, 2, 1, 3)
    own_pages = page_table[:, own_page0:]
    k_own = jnp.concatenate([cache_k[own_pages].reshape(DB, n_own_past, H, D).astype(q.dtype), k.astype(q.dtype)], axis=1).transpose(0, 2, 1, 3)
    v_own = jnp.concatenate([cache_v[own_pages].reshape(DB, n_own_past, H, D).astype(v.dtype), v], axis=1).transpose(0, 2, 1, 3)
    own_ok = jnp.arange(n_own_past + T)[None, :] <= (n_own_past + jnp.arange(T))[:, None]
    s_own = jnp.where(own_ok, jnp.einsum('bhqd,bhkd->bhqk', qt, k_own, preferred_element_type=F32) * scale, NEG_INF)
    if n_sel == 0:
        p = jax.nn.softmax(s_own, axis=-1).astype(v.dtype)
        return jnp.einsum('bhqk,bhkd->bqhd', p, v_own)
    page_sum = jnp.sum(cache_k.astype(F32), axis=1)
    kmean = page_sum[page_table[:, :own_page0]].reshape(DB, n_full, ppb, H, D).sum(axis=2) / MOBA_BLOCK
    gate = jnp.einsum('bhqd,bnhd->bhqn', qt.astype(F32), kmean)
    _, idx = lax.top_k(gate, n_sel)
    bi = jnp.arange(DB)[:, None, None, None, None]
    phys = page_table[bi, idx[..., None] * ppb + jnp.arange(ppb)]
    hi = jnp.arange(H)[None, :, None, None, None]
    k_sel = cache_k[phys, :, hi, :].reshape(DB, H, T, n_sel, MOBA_BLOCK, D).astype(q.dtype)
    v_sel = cache_v[phys, :, hi, :].reshape(DB, H, T, n_sel, MOBA_BLOCK, D).astype(v.dtype)
    s_sel = (jnp.einsum('bhqd,bhqjkd->bhqjk', qt, k_sel, preferred_element_type=F32) * scale).reshape(DB, H, T, n_sel * MOBA_BLOCK)
    p = jax.nn.softmax(jnp.concatenate([s_sel, s_own], axis=-1), axis=-1).astype(v.dtype)
    p_sel = p[..., :n_sel * MOBA_BLOCK].reshape(DB, H, T, n_sel, MOBA_BLOCK)
    p_own = p[..., n_sel * MOBA_BLOCK:]
    return jnp.einsum('bhqjk,bhqjkd->bqhd', p_sel, v_sel) + jnp.einsum('bhqk,bhkd->bqhd', p_own, v_own)


def token_mixer(h, shift_prev, wkv_prev, pos, attend, lp):
    B, T, _ = h.shape
    proj = h @ lp['w_in']
    p_prev = shift_prev @ lp['w_in'][:, :C_RWKV_IN]
    r, decay, k, v, kk, a, g = rwkv_inputs(proj[..., :C_RWKV_IN], p_prev, lp)
    y, wkv_new = wkv_scan(r, decay, k, v, kk, a, wkv_prev.astype(F32))
    o_rwkv = rwkv_output(y, r, k, v, g, lp)
    qkv = proj[..., C_RWKV_IN:].reshape(B, T, 3, H_MOBA, HEAD_DIM)
    q = rope(qkv[:, :, 0], pos)
    km = rope(qkv[:, :, 1], pos)
    vm = qkv[:, :, 2]
    o_moba = attend(q, km, vm).reshape(B, T, D_MOBA)
    o = jnp.concatenate([o_rwkv.astype(h.dtype), o_moba.astype(h.dtype)], axis=-1) @ lp['w_out']
    return o, km, vm, wkv_new, h[:, -1]


def setup_inputs(seed: int = 0) -> dict:
    key = jax.random.key(seed)
    ks = jax.random.split(key, 30)
    nrm = lambda i, shape, scale: scale * jax.random.normal(ks[i], shape, F32)
    gain = lambda i, shape: 1.0 + nrm(i, shape, 0.05)
    n_pages = PAST_LEN // PAGE_SIZE
    n_used = DEC_BATCH * n_pages
    n_pool = n_used + n_used // 4
    page_table = jax.random.permutation(ks[6], n_pool)[:n_used].reshape(DEC_BATCH, n_pages).astype(jnp.int32)
    kv_shape = (DEPTH, n_pool, PAGE_SIZE, H_MOBA, HEAD_DIM)
    return {
        'x_prompt': nrm(0, (BATCH, SEQ, D_MODEL), 1.0),
        'x_sample': nrm(1, (DEC_BATCH, DEC_SEQ, D_MODEL), 1.0),
        'cache_k': nrm(2, kv_shape, 1.0),
        'cache_v': nrm(3, kv_shape, 1.0),
        'state_wkv': nrm(4, (DEPTH, DEC_BATCH, H_RWKV, HEAD_DIM, HEAD_DIM), 0.3),
        'state_shift': nrm(5, (DEPTH, DEC_BATCH, D_MODEL), 1.0),
        'page_table': page_table,
        'ffn1_norm': gain(7, (DEPTH, D_MODEL)),
        'ffn1_w_gate': nrm(8, (DEPTH, D_MODEL, D_FF), D_MODEL ** -0.5),
        'ffn1_w_up': nrm(9, (DEPTH, D_MODEL, D_FF), D_MODEL ** -0.5),
        'ffn1_w_down': nrm(10, (DEPTH, D_FF, D_MODEL), D_FF ** -0.5),
        'mix_norm': gain(11, (DEPTH, D_MODEL)),
        'w_in': nrm(12, (DEPTH, D_MODEL, C_IN), D_MODEL ** -0.5),
        'w_out': nrm(13, (DEPTH, D_MIX, D_MODEL), D_MIX ** -0.5),
        'rwkv_mu': jax.random.uniform(ks[14], (DEPTH, C_RWKV_IN), F32),
        'rwkv_w0': jax.random.uniform(ks[15], (DEPTH, D_RWKV), F32, -6.0, -1.0),
        'rwkv_w2': nrm(16, (DEPTH, DECAY_LORA, D_RWKV), 0.5 * DECAY_LORA ** -0.5),
        'rwkv_a0': nrm(17, (DEPTH, D_RWKV), 0.1),
        'rwkv_a2': nrm(18, (DEPTH, ICLR_LORA, D_RWKV), 0.5 * ICLR_LORA ** -0.5),
        'rwkv_g2': nrm(19, (DEPTH, GATE_LORA, D_RWKV), GATE_LORA ** -0.5),
        'rwkv_k_k': 0.85 + nrm(20, (DEPTH, D_RWKV), 0.05),
        'rwkv_k_a': 1.0 + nrm(21, (DEPTH, D_RWKV), 0.05),
        'rwkv_r_k': nrm(22, (DEPTH, H_RWKV, HEAD_DIM), 0.1),
        'rwkv_ln_w': gain(23, (DEPTH, D_RWKV)),
        'rwkv_ln_b': nrm(24, (DEPTH, D_RWKV), 0.02),
        'ffn2_norm': gain(25, (DEPTH, D_MODEL)),
        'ffn2_w_gate': nrm(26, (DEPTH, D_MODEL, D_FF), D_MODEL ** -0.5),
        'ffn2_w_up': nrm(27, (DEPTH, D_MODEL, D_FF), D_MODEL ** -0.5),
        'ffn2_w_down': nrm(28, (DEPTH, D_FF, D_MODEL), D_FF ** -0.5),
        'final_norm': gain(29, (D_MODEL,)),
    }


def reference(x_prompt, x_sample, cache_k, cache_v, state_wkv, state_shift, page_table,
              ffn1_norm, ffn1_w_gate, ffn1_w_up, ffn1_w_down, mix_norm, w_in, w_out,
              rwkv_mu, rwkv_w0, rwkv_w2, rwkv_a0, rwkv_a2, rwkv_g2, rwkv_k_k, rwkv_k_a, rwkv_r_k,
              rwkv_ln_w, rwkv_ln_b, ffn2_norm, ffn2_w_gate, ffn2_w_up, ffn2_w_down, final_norm):
    B, T = x_prompt.shape[0], x_prompt.shape[1]
    pos_p = jnp.arange(T)
    pos_s = PAST_LEN + jnp.arange(x_sample.shape[1])
    yp, ys = x_prompt, x_sample
    kp_l, vp_l, ks_l, vs_l, wp_l, ws_l, sp_l, ss_l = [], [], [], [], [], [], [], []
    for l in range(DEPTH):
        lp = {'w_in': w_in[l], 'w_out': w_out[l], 'mu': rwkv_mu[l], 'w0': rwkv_w0[l], 'w2': rwkv_w2[l],
              'a0': rwkv_a0[l], 'a2': rwkv_a2[l], 'g2': rwkv_g2[l], 'k_k': rwkv_k_k[l], 'k_a': rwkv_k_a[l],
              'r_k': rwkv_r_k[l], 'ln_w': rwkv_ln_w[l], 'ln_b': rwkv_ln_b[l]}
        yp = ffn_half(yp, ffn1_norm[l], ffn1_w_gate[l], ffn1_w_up[l], ffn1_w_down[l])
        ys = ffn_half(ys, ffn1_norm[l], ffn1_w_gate[l], ffn1_w_up[l], ffn1_w_down[l])
        hp = rms_norm(yp, mix_norm[l])
        hs = rms_norm(ys, mix_norm[l])
        op, kp, vp, wp, sp = token_mixer(hp, jnp.zeros((B, D_MODEL), hp.dtype),
                                         jnp.zeros((B, H_RWKV, HEAD_DIM, HEAD_DIM), F32),
                                         pos_p, moba_prompt, lp)
        attend_s = functools.partial(moba_sample, cache_k[l], cache_v[l], page_table)
        os_, ks, vs, ws, ss = token_mixer(hs, state_shift[l], state_wkv[l], pos_s, attend_s, lp)
        yp = yp + op
        ys = ys + os_
        yp = ffn_half(yp, ffn2_norm[l], ffn2_w_gate[l], ffn2_w_up[l], ffn2_w_down[l])
        ys = ffn_half(ys, ffn2_norm[l], ffn2_w_gate[l], ffn2_w_up[l], ffn2_w_down[l])
        kp_l.append(kp); vp_l.append(vp); ks_l.append(ks); vs_l.append(vs)
        wp_l.append(wp.astype(state_wkv.dtype)); ws_l.append(ws.astype(state_wkv.dtype))
        sp_l.append(sp); ss_l.append(ss.astype(state_shift.dtype))
    y_prompt = rms_norm(yp, final_norm)
    y_sample = rms_norm(ys, final_norm)
    return (y_prompt, y_sample, jnp.stack(kp_l), jnp.stack(vp_l), jnp.stack(ks_l), jnp.stack(vs_l),
            jnp.stack(wp_l), jnp.stack(ws_l), jnp.stack(sp_l), jnp.stack(ss_l))
```

```python
import functools
import math

import jax
import jax.numpy as jnp
from jax import lax
from jax.experimental import pallas as pl
from jax.experimental.pallas import tpu as pltpu

F32 = jnp.float32
BF16 = jnp.bfloat16
HI = lax.Precision.HIGHEST

HEAD_DIM = 64
MOBA_BLOCK = 256
MOBA_TOPK = 3
ROPE_THETA = 10000.0
NORM_EPS = 1e-6
GN_EPS = 64e-5
NEG_INF = -1e30
DECAY_LORA = 64
ICLR_LORA = 64
GATE_LORA = 128
LANES = 128

VMEM_LIMIT = 56 * 1024 * 1024


def _cparams(*sem):
    return pltpu.CompilerParams(dimension_semantics=sem, vmem_limit_bytes=VMEM_LIMIT)


def _row_tile(n, target):
    t = min(n, target)
    while n % t:
        t //= 2
    return t


def _nt_dot(a, b, precision=None):
    return lax.dot_general(a, b, (((1,), (1,)), ((), ())), precision=precision, preferred_element_type=F32)


def _ffn_kernel(x_ref, g_ref, wg_ref, wu_ref, wd_ref, o_ref, h_scr, acc_scr):
    j = pl.program_id(1)

    @pl.when(j == 0)
    def _():
        x = x_ref[...]
        ms = jnp.mean(x * x, axis=-1, keepdims=True)
        h_scr[...] = (x * lax.rsqrt(ms + NORM_EPS) * g_ref[...]).astype(BF16)
        acc_scr[...] = jnp.zeros_like(acc_scr)

    h = h_scr[...]
    a = jnp.dot(h, wg_ref[...], preferred_element_type=F32)
    b = jnp.dot(h, wu_ref[...], preferred_element_type=F32)
    u = a * jax.nn.sigmoid(a) * b
    acc_scr[...] += jnp.dot(u.astype(BF16), wd_ref[...], preferred_element_type=F32)

    @pl.when(j == pl.num_programs(1) - 1)
    def _():
        o_ref[...] = x_ref[...] + 0.5 * acc_scr[...]


def ffn_half(x, g, wg, wu, wd, *, tm_target=1024, tf=256):
    n, d = x.shape
    f = wg.shape[1]
    tm = _row_tile(n, tm_target)
    assert f % tf == 0
    return pl.pallas_call(
        _ffn_kernel,
        grid=(n // tm, f // tf),
        in_specs=[
            pl.BlockSpec((tm, d), lambda i, j: (i, 0)),
            pl.BlockSpec((1, d), lambda i, j: (0, 0)),
            pl.BlockSpec((d, tf), lambda i, j: (0, j)),
            pl.BlockSpec((d, tf), lambda i, j: (0, j)),
            pl.BlockSpec((tf, d), lambda i, j: (j, 0)),
        ],
        out_specs=pl.BlockSpec((tm, d), lambda i, j: (i, 0)),
        out_shape=jax.ShapeDtypeStruct((n, d), F32),
        scratch_shapes=[pltpu.VMEM((tm, d), BF16), pltpu.VMEM((tm, d), F32)],
        compiler_params=_cparams("parallel", "arbitrary"),
        name="ffn_half",
    )(x, g.reshape(1, d), wg, wu, wd)


def _proj_kernel(x_ref, g_ref, w_ref, o_ref, h_scr, *, normalize):
    @pl.when(pl.program_id(1) == 0)
    def _():
        x = x_ref[...]
        if normalize:
            ms = jnp.mean(x * x, axis=-1, keepdims=True)
            x = x * lax.rsqrt(ms + NORM_EPS) * g_ref[...]
        h_scr[...] = x.astype(BF16)

    o_ref[...] = jnp.dot(h_scr[...], w_ref[...], preferred_element_type=F32)


def norm_proj(x, g, w, *, normalize=True, tm_target=1024, tn=256):
    n, d = x.shape
    c = w.shape[1]
    tm = _row_tile(n, tm_target)
    assert c % tn == 0
    return pl.pallas_call(
        functools.partial(_proj_kernel, normalize=normalize),
        grid=(n // tm, c // tn),
        in_specs=[
            pl.BlockSpec((tm, d), lambda i, j: (i, 0)),
            pl.BlockSpec((1, d), lambda i, j: (0, 0)),
            pl.BlockSpec((d, tn), lambda i, j: (0, j)),
        ],
        out_specs=pl.BlockSpec((tm, tn), lambda i, j: (i, j)),
        out_shape=jax.ShapeDtypeStruct((n, c), F32),
        scratch_shapes=[pltpu.VMEM((tm, d), BF16)],
        compiler_params=_cparams("parallel", "arbitrary"),
        name="norm_proj",
    )(x, g.reshape(1, d), w)


def _rms_kernel(x_ref, g_ref, o_ref):
    x = x_ref[...]
    ms = jnp.mean(x * x, axis=-1, keepdims=True)
    o_ref[...] = x * lax.rsqrt(ms + NORM_EPS) * g_ref[...]


def rms_norm_rows(x, g, *, tm_target=1024):
    n, d = x.shape
    tm = _row_tile(n, tm_target)
    return pl.pallas_call(
        _rms_kernel,
        grid=(n // tm,),
        in_specs=[pl.BlockSpec((tm, d), lambda i: (i, 0)), pl.BlockSpec((1, d), lambda i: (0, 0))],
        out_specs=pl.BlockSpec((tm, d), lambda i: (i, 0)),
        out_shape=jax.ShapeDtypeStruct((n, d), F32),
        compiler_params=_cparams("parallel"),
        name="rms_norm",
    )(x, g.reshape(1, d))


def _head_block_diag(n):
    r = lax.broadcasted_iota(jnp.int32, (n, n), 0) // HEAD_DIM
    c = lax.broadcasted_iota(jnp.int32, (n, n), 1) // HEAD_DIM
    return (r == c).astype(F32)


def _prep_kernel(*refs, has_prev_tile, n_heads):
    if has_prev_tile:
        p_ref, prev8_ref, pprev_ref = refs[:3]
        rest = refs[3:]
    else:
        p_ref, pprev_ref = refs[:2]
        prev8_ref = None
        rest = refs[2:]
    (mu_ref, w0_ref, w2_ref, a0_ref, a2_ref, g2_ref, kk_ref, ka_ref,
     r_out, w_out, k_out, v_out, kk_out, kka_out, g_out) = rest
    d_r = n_heads * HEAD_DIM
    p = p_ref[0]
    tt = p.shape[0]
    prev_row = pprev_ref[0]
    if has_prev_tile:
        prev_row = jnp.where(pl.program_id(1) == 0, prev_row, prev8_ref[0, 7:8, :])
    row = lax.broadcasted_iota(jnp.int32, p.shape, 0)
    if tt % 8 == 0:
        rolled = pltpu.roll(p, 1, 0)
    else:
        rolled = jnp.concatenate([p[tt - 1:], p[:tt - 1]], axis=0)
    p_shift = jnp.where(row == 0, prev_row, rolled)
    xm = p + (p_shift - p) * mu_ref[...]
    r = xm[:, :d_r]
    k = xm[:, d_r:2 * d_r]
    v = xm[:, 2 * d_r:3 * d_r]
    c0 = 3 * d_r
    xw = xm[:, c0:c0 + DECAY_LORA]
    xa = xm[:, c0 + DECAY_LORA:c0 + DECAY_LORA + ICLR_LORA]
    xg = xm[:, c0 + DECAY_LORA + ICLR_LORA:]
    z = -(w0_ref[...] + jnp.dot(jnp.tanh(xw), w2_ref[...], precision=HI, preferred_element_type=F32))
    softplus = jnp.maximum(z, 0.0) + jnp.log(1.0 + jnp.exp(-jnp.abs(z)))
    decay = jnp.exp(-jnp.exp(-softplus - 0.5))
    a = jax.nn.sigmoid(a0_ref[...] + jnp.dot(xa, a2_ref[...], precision=HI, preferred_element_type=F32))
    g = jnp.dot(jax.nn.sigmoid(xg), g2_ref[...], precision=HI, preferred_element_type=F32)
    kk = k * kk_ref[...]
    ss = jnp.dot(kk * kk, _head_block_diag(d_r), precision=HI, preferred_element_type=F32)
    kk = kk / jnp.maximum(jnp.sqrt(ss), 1e-12)
    k2 = k * (1.0 + (a - 1.0) * ka_ref[...])
    kka = kk * a
    for h in range(n_heads):
        sl = slice(h * HEAD_DIM, (h + 1) * HEAD_DIM)
        r_out[0, h] = r[:, sl]
        w_out[0, h] = decay[:, sl]
        k_out[0, h] = k2[:, sl]
        v_out[0, h] = v[:, sl]
        kk_out[0, h] = kk[:, sl]
        kka_out[0, h] = kka[:, sl]
        g_out[0, h] = g[:, sl]


def rwkv_prep(pr, p_prev, lp, *, tt_target=256):
    b, t, c = pr.shape
    d_r = lp['w0'].shape[0]
    n_heads = d_r // HEAD_DIM
    tt = _row_tile(t, tt_target)
    nt = t // tt
    has_prev = nt > 1
    row = lambda x: x.reshape(1, -1)
    full = lambda shape: pl.BlockSpec(shape, lambda bi, i: (0,) * len(shape))
    in_specs = [pl.BlockSpec((1, tt, c), lambda bi, i: (bi, i, 0))]
    args = [pr]
    if has_prev:
        in_specs.append(pl.BlockSpec((1, 8, c), lambda bi, i: (bi, jnp.maximum(i * (tt // 8) - 1, 0), 0)))
        args.append(pr)
    in_specs.append(pl.BlockSpec((1, 1, c), lambda bi, i: (bi, 0, 0)))
    args.append(p_prev.reshape(b, 1, c))
    params = [row(lp['mu']), row(lp['w0']), lp['w2'], row(lp['a0']), lp['a2'], lp['g2'], row(lp['k_k']), row(lp['k_a'])]
    in_specs += [full(x.shape) for x in params]
    args += params
    hm = jax.ShapeDtypeStruct((b, n_heads, t, HEAD_DIM), F32)
    hm_spec = pl.BlockSpec((1, n_heads, tt, HEAD_DIM), lambda bi, i: (bi, 0, i, 0))
    return pl.pallas_call(
        functools.partial(_prep_kernel, has_prev_tile=has_prev, n_heads=n_heads),
        grid=(b, nt),
        in_specs=in_specs,
        out_specs=[hm_spec] * 7,
        out_shape=[hm] * 7,
        compiler_params=_cparams("parallel", "arbitrary"),
        name="rwkv_prep",
    )(*args)


def _scan_kernel(r_ref, w_ref, k_ref, v_ref, kk_ref, kka_ref, s0_ref, y_ref, s_out_ref, s_scr, *, n_heads, tc):
    @pl.when(pl.program_id(1) == 0)
    def _():
        s_scr[...] = s0_ref[0]

    eye = (lax.broadcasted_iota(jnp.int32, (HEAD_DIM, HEAD_DIM), 0)
           == lax.broadcasted_iota(jnp.int32, (HEAD_DIM, HEAD_DIM), 1))

    def body(t, carry):
        for h in range(n_heads):
            s = s_scr[h]
            row = lambda ref: ref[0, h, pl.ds(t, 1), :]
            sa = -jnp.sum(s * row(kk_ref), axis=1, keepdims=True)
            v_col = jnp.sum(jnp.where(eye, row(v_ref), 0.0), axis=1, keepdims=True)
            s = s * row(w_ref) + sa * row(kka_ref) + v_col * row(k_ref)
            s_scr[h] = s
            y_col = jnp.sum(s * row(r_ref), axis=1, keepdims=True)
            y_ref[0, h, pl.ds(t, 1), :] = jnp.sum(jnp.where(eye, y_col, 0.0), axis=0, keepdims=True)
        return carry

    lax.fori_loop(0, tc, body, 0)
    s_out_ref[0] = s_scr[...]


def wkv_scan(r, w, k, v, kk, kka, s0, *, tc_target=256):
    b, h, t, n = r.shape
    tc = _row_tile(t, tc_target)
    seq = pl.BlockSpec((1, h, tc, n), lambda bi, c: (bi, 0, c, 0))
    st = pl.BlockSpec((1, h, n, n), lambda bi, c: (bi, 0, 0, 0))
    return pl.pallas_call(
        functools.partial(_scan_kernel, n_heads=h, tc=tc),
        grid=(b, t // tc),
        in_specs=[seq] * 6 + [st],
        out_specs=[seq, st],
        out_shape=[jax.ShapeDtypeStruct((b, h, t, n), F32), jax.ShapeDtypeStruct((b, h, n, n), F32)],
        scratch_shapes=[pltpu.VMEM((h, n, n), F32)],
        compiler_params=_cparams("parallel", "arbitrary"),
        name="wkv_scan",
    )(r, w, k, v, kk, kka, s0)


def _rope_kernel(p_ref, cos_ref, sin_ref, krow_ref, q_out, k_out, v_out, *, n_heads):
    d_m = n_heads * HEAD_DIM
    half = HEAD_DIM // 2
    cos = cos_ref[...]
    sin = sin_ref[...]
    first_half = (lax.broadcasted_iota(jnp.int32, cos.shape, 1) % HEAD_DIM) < half

    def rot(x):
        if x.shape[0] % 8 == 0:
            fwd = pltpu.roll(x, d_m - half, 1)
            bwd = pltpu.roll(x, half, 1)
        else:
            fwd = jnp.concatenate([x[:, half:], x[:, :half]], axis=1)
            bwd = jnp.concatenate([x[:, d_m - half:], x[:, :d_m - half]], axis=1)
        return x * cos + jnp.where(first_half, fwd, bwd) * sin

    p = p_ref[0]
    q = rot(p[:, :d_m])
    k = rot(p[:, d_m:2 * d_m])
    v = p[:, 2 * d_m:]
    krow_ref[0] = k
    for h in range(n_heads):
        sl = slice(h * HEAD_DIM, (h + 1) * HEAD_DIM)
        q_out[0, h] = q[:, sl]
        k_out[0, h] = k[:, sl].astype(BF16)
        v_out[0, h] = v[:, sl].astype(BF16)


def rope_qkv(pm, cos, sin, *, tt_target=512):
    b, t, c = pm.shape
    d_m = c // 3
    n_heads = d_m // HEAD_DIM
    tt = _row_tile(t, tt_target)
    hm_spec = pl.BlockSpec((1, n_heads, tt, HEAD_DIM), lambda bi, i: (bi, 0, i, 0))
    hm = lambda dt: jax.ShapeDtypeStruct((b, n_heads, t, HEAD_DIM), dt)
    return pl.pallas_call(
        functools.partial(_rope_kernel, n_heads=n_heads),
        grid=(b, t // tt),
        in_specs=[
            pl.BlockSpec((1, tt, c), lambda bi, i: (bi, i, 0)),
            pl.BlockSpec((tt, d_m), lambda bi, i: (i, 0)),
            pl.BlockSpec((tt, d_m), lambda bi, i: (i, 0)),
        ],
        out_specs=[pl.BlockSpec((1, tt, d_m), lambda bi, i: (bi, i, 0)), hm_spec, hm_spec, hm_spec],
        out_shape=[jax.ShapeDtypeStruct((b, t, d_m), F32), hm(F32), hm(BF16), hm(BF16)],
        compiler_params=_cparams("parallel", "parallel"),
        name="rope_qkv",
    )(pm, cos, sin)


def rope_tables(pos, n_heads):
    half = HEAD_DIM // 2
    inv_freq = ROPE_THETA ** (-jnp.arange(half, dtype=F32) / half)
    ang = pos.astype(F32)[:, None] * inv_freq[None, :]
    cos = jnp.cos(ang)
    sin = jnp.sin(ang)
    cos = jnp.tile(jnp.concatenate([cos, cos], axis=1), (1, n_heads))
    sin = jnp.tile(jnp.concatenate([-sin, sin], axis=1), (1, n_heads))
    return cos, sin


def _kmean_kernel(k_ref, o_ref, *, nblk):
    for j in range(nblk):
        o_ref[0, j:j + 1, :] = jnp.mean(k_ref[0, j * MOBA_BLOCK:(j + 1) * MOBA_BLOCK, :], axis=0, keepdims=True)


def block_means(k_rows):
    b, t, d_m = k_rows.shape
    nb = t // MOBA_BLOCK
    nblk = 8 if nb % 8 == 0 else nb
    return pl.pallas_call(
        functools.partial(_kmean_kernel, nblk=nblk),
        grid=(b, nb // nblk),
        in_specs=[pl.BlockSpec((1, nblk * MOBA_BLOCK, d_m), lambda bi, i: (bi, i, 0))],
        out_specs=pl.BlockSpec((1, nblk, d_m), lambda bi, i: (bi, i, 0)),
        out_shape=jax.ShapeDtypeStruct((b, nb, d_m), F32),
        compiler_params=_cparams("parallel", "parallel"),
        name="block_means",
    )(k_rows)


def _top_blocks(gate, valid, n_sel):
    nb = gate.shape[1]
    blk = lax.broadcasted_iota(jnp.int32, gate.shape, 1)
    avail = valid
    sel = jnp.zeros(gate.shape, F32)
    for _ in range(n_sel):
        g = jnp.where(avail, gate, -jnp.inf)
        m = jnp.max(g, axis=1, keepdims=True)
        first = jnp.min(jnp.where((g == m) & avail, blk, nb), axis=1, keepdims=True)
        pick = blk == first
        sel = jnp.where(pick, 1.0, sel)
        avail = avail & jnp.logical_not(pick)
    return sel


def _moba_prompt_kernel(q_ref, k_ref, v_ref, km_ref, o_ref, *, n_sel):
    i = pl.program_id(2)
    blk_q = q_ref.shape[2]
    nb = km_ref.shape[2]
    scale = HEAD_DIM ** -0.5
    q32 = q_ref[0, 0]
    q = q32.astype(BF16)

    def scores(j):
        kj = k_ref[0, 0, pl.ds(pl.multiple_of(j * MOBA_BLOCK, MOBA_BLOCK), MOBA_BLOCK), :]
        return _nt_dot(q, kj) * scale

    def values(j):
        return v_ref[0, 0, pl.ds(pl.multiple_of(j * MOBA_BLOCK, MOBA_BLOCK), MOBA_BLOCK), :]

    qi = lax.broadcasted_iota(jnp.int32, (blk_q, MOBA_BLOCK), 0)
    ki = lax.broadcasted_iota(jnp.int32, (blk_q, MOBA_BLOCK), 1)
    s = jnp.where(ki <= qi, scores(i), NEG_INF)
    m0 = jnp.max(s, axis=1, keepdims=True)
    p = jnp.exp(s - m0)
    l0 = jnp.sum(p, axis=1, keepdims=True)
    acc0 = jnp.dot(p.astype(BF16), values(i), preferred_element_type=F32)

    if n_sel == 0:
        o_ref[0, 0] = acc0 / l0
        return

    gate = _nt_dot(q32, km_ref[0, 0], precision=HI)
    blk = lax.broadcasted_iota(jnp.int32, (blk_q, nb), 1)
    sel = _top_blocks(gate, blk < i, n_sel)

    def body(j, carry):
        m, l, acc = carry
        col = jnp.sum(jnp.where(blk == j, sel, 0.0), axis=1, keepdims=True)
        s = jnp.where(col > 0.0, scores(j), NEG_INF)
        m_new = jnp.maximum(m, jnp.max(s, axis=1, keepdims=True))
        alpha = jnp.exp(m - m_new)
        p = jnp.exp(s - m_new)
        l = alpha * l + jnp.sum(p, axis=1, keepdims=True)
        acc = alpha * acc + jnp.dot(p.astype(BF16), values(j), preferred_element_type=F32)
        return m_new, l, acc

    m, l, acc = lax.fori_loop(0, i, body, (m0, l0, acc0))
    o_ref[0, 0] = acc / l


def moba_prompt(q, k, v, kmean):
    b, h, t, d = q.shape
    assert t % MOBA_BLOCK == 0
    nb = t // MOBA_BLOCK
    n_sel = min(MOBA_TOPK, nb - 1)
    nb_pad = -(-nb // LANES) * LANES
    kmean = jnp.pad(kmean, ((0, 0), (0, 0), (0, nb_pad - nb), (0, 0)))
    kv_spec = pl.BlockSpec((1, 1, t, d), lambda bi, hi, i: (bi, hi, 0, 0))
    return pl.pallas_call(
        functools.partial(_moba_prompt_kernel, n_sel=n_sel),
        grid=(b, h, nb),
        in_specs=[
            pl.BlockSpec((1, 1, MOBA_BLOCK, d), lambda bi, hi, i: (bi, hi, i, 0)),
            kv_spec,
            kv_spec,
            pl.BlockSpec((1, 1, nb_pad, d), lambda bi, hi, i: (bi, hi, 0, 0)),
        ],
        out_specs=pl.BlockSpec((1, 1, MOBA_BLOCK, d), lambda bi, hi, i: (bi, hi, i, 0)),
        out_shape=jax.ShapeDtypeStruct((b, h, t, d), F32),
        compiler_params=_cparams("parallel", "parallel", "arbitrary"),
        name="moba_prompt",
    )(q, k, v, kmean)


PAGES_PER_STEP = 16


def _sample_kmean_kernel(pt_ref, *refs, ppb, page_size):
    pages, o_ref = refs[:-1], refs[-1]
    nblk = len(pages) // ppb
    for j in range(nblk):
        s = jnp.sum(pages[j * ppb][0, 0], axis=0, keepdims=True)
        for q in range(1, ppb):
            s = s + jnp.sum(pages[j * ppb + q][0, 0], axis=0, keepdims=True)
        o_ref[0, j:j + 1, :] = s / (ppb * page_size)


def sample_block_means(cache_k, layer, page_table):
    _, _, page_size, d_m = cache_k.shape
    db, n_pages = page_table.shape
    ppb = MOBA_BLOCK // page_size
    n_full = (n_pages * page_size) // MOBA_BLOCK
    pps = PAGES_PER_STEP
    assert n_pages % pps == 0 and (pps // ppb) % 8 == 0

    def page_spec(q):
        return pl.BlockSpec((1, 1, page_size, d_m), lambda bi, g, pt: (layer, pt[bi, g * pps + q], 0, 0))

    return pl.pallas_call(
        functools.partial(_sample_kmean_kernel, ppb=ppb, page_size=page_size),
        grid_spec=pltpu.PrefetchScalarGridSpec(
            num_scalar_prefetch=1,
            grid=(db, n_pages // pps),
            in_specs=[page_spec(q) for q in range(pps)],
            out_specs=pl.BlockSpec((1, pps // ppb, d_m), lambda bi, g, pt: (bi, g, 0)),
        ),
        out_shape=jax.ShapeDtypeStruct((db, n_full, d_m), F32),
        compiler_params=_cparams("parallel", "arbitrary"),
        name="sample_block_means",
    )(page_table, *([cache_k] * pps))


def _moba_sample_kernel(pt_ref, q_ref, kn_ref, vn_ref, km_ref, *refs, ppb, n_heads, n_sel):
    k_pages = refs[:ppb]
    v_pages = refs[ppb:2 * ppb]
    o_ref = refs[2 * ppb]
    qbd_scr, sel_scr, m_scr, l_scr, acc_scr = refs[2 * ppb + 1:]
    n = pl.program_id(1)
    rows = q_ref.shape[1]
    t = rows // n_heads
    d_m = n_heads * HEAD_DIM
    nb = km_ref.shape[1]
    n_full = pl.num_programs(1)
    scale = HEAD_DIM ** -0.5
    row_head = lax.broadcasted_iota(jnp.int32, (rows, d_m), 0) // t
    col_head = lax.broadcasted_iota(jnp.int32, (rows, d_m), 1) // HEAD_DIM
    own_head = row_head == col_head

    @pl.when(n == 0)
    def _():
        qbd = jnp.where(own_head, q_ref[0], 0.0)
        qbd_scr[...] = qbd
        if n_sel > 0:
            gate = _nt_dot(qbd, km_ref[0], precision=HI)
            blk = lax.broadcasted_iota(jnp.int32, gate.shape, 1)
            sel_scr[...] = _top_blocks(gate, blk < n_full, n_sel)
        s = _nt_dot(qbd.astype(BF16), kn_ref[0].astype(BF16)) * scale
        q_t = lax.broadcasted_iota(jnp.int32, s.shape, 0) % t
        k_t = lax.broadcasted_iota(jnp.int32, s.shape, 1)
        s = jnp.where(k_t <= q_t, s, NEG_INF)
        m = jnp.max(s, axis=1, keepdims=True)
        p = jnp.exp(s - m)
        m_scr[...] = m
        l_scr[...] = jnp.sum(p, axis=1, keepdims=True)
        acc_scr[...] = jnp.dot(p.astype(BF16), vn_ref[0].astype(BF16), preferred_element_type=F32)

    if n_sel > 0:
        blk = lax.broadcasted_iota(jnp.int32, (rows, nb), 1)
        col = jnp.sum(jnp.where(blk == n, sel_scr[...], 0.0), axis=1, keepdims=True)
        qb = qbd_scr[...].astype(BF16)
        for pg in range(ppb):
            s = jnp.where(col > 0.0, _nt_dot(qb, k_pages[pg][0, 0].astype(BF16)) * scale, NEG_INF)
            m = m_scr[...]
            m_new = jnp.maximum(m, jnp.max(s, axis=1, keepdims=True))
            alpha = jnp.exp(m - m_new)
            p = jnp.exp(s - m_new)
            m_scr[...] = m_new
            l_scr[...] = alpha * l_scr[...] + jnp.sum(p, axis=1, keepdims=True)
            acc_scr[...] = alpha * acc_scr[...] + jnp.dot(p.astype(BF16), v_pages[pg][0, 0].astype(BF16),
                                                           preferred_element_type=F32)

    @pl.when(n == pl.num_programs(1) - 1)
    def _():
        o_ref[0] = acc_scr[...] / l_scr[...]


def moba_sample(cache_k, cache_v, layer, page_table, q_rows, k_rows, v_rows, kmean):
    _, _, page_size, d_m = cache_k.shape
    db, t, _ = q_rows.shape
    n_pages = page_table.shape[1]
    n_heads = d_m // HEAD_DIM
    ppb = MOBA_BLOCK // page_size
    n_full = (n_pages * page_size) // MOBA_BLOCK
    assert n_full * ppb == n_pages, "past rows inside the current block are not supported"
    n_sel = min(MOBA_TOPK, n_full)
    assert n_sel > 0 and t <= page_size
    rows = n_heads * t
    nb_pad = -(-n_full // LANES) * LANES
    q_rep = jnp.tile(q_rows, (1, n_heads, 1))
    pad_rows = lambda x: jnp.pad(x, ((0, 0), (0, page_size - t), (0, 0)))
    km_pad = jnp.pad(kmean, ((0, 0), (0, nb_pad - n_full), (0, 0)))
    q_spec = pl.BlockSpec((1, rows, d_m), lambda bi, n, pt: (bi, 0, 0))
    new_spec = pl.BlockSpec((1, page_size, d_m), lambda bi, n, pt: (bi, 0, 0))

    def page_spec(q):
        return pl.BlockSpec((1, 1, page_size, d_m), lambda bi, n, pt: (layer, pt[bi, n * ppb + q], 0, 0))

    o = pl.pallas_call(
        functools.partial(_moba_sample_kernel, ppb=ppb, n_heads=n_heads, n_sel=n_sel),
        grid_spec=pltpu.PrefetchScalarGridSpec(
            num_scalar_prefetch=1,
            grid=(db, n_full),
            in_specs=[q_spec, new_spec, new_spec, pl.BlockSpec((1, nb_pad, d_m), lambda bi, n, pt: (bi, 0, 0))]
            + [page_spec(q) for q in range(ppb)] * 2,
            out_specs=q_spec,
            scratch_shapes=[
                pltpu.VMEM((rows, d_m), F32),
                pltpu.VMEM((rows, nb_pad), F32),
                pltpu.VMEM((rows, 1), F32),
                pltpu.VMEM((rows, 1), F32),
                pltpu.VMEM((rows, d_m), F32),
            ],
        ),
        out_shape=jax.ShapeDtypeStruct((db, rows, d_m), F32),
        compiler_params=_cparams("parallel", "arbitrary"),
        name="moba_sample",
    )(page_table, q_rep, pad_rows(k_rows), pad_rows(v_rows), km_pad, *([cache_k] * ppb), *([cache_v] * ppb))
    o = o.reshape(db, n_heads, t, n_heads, HEAD_DIM)
    return jnp.stack([o[:, h, :, h, :] for h in range(n_heads)], axis=1)


def _mix_out_kernel(y_ref, r_ref, k_ref, v_ref, g_ref, om_ref, x_ref, wo_ref, lnw_ref, lnb_ref, rk_ref, o_ref,
                    *, h_rwkv, h_moba):
    acc = x_ref[0]
    for h in range(h_rwkv):
        y = y_ref[0, h]
        mean = jnp.mean(y, axis=-1, keepdims=True)
        yc = y - mean
        var = jnp.mean(yc * yc, axis=-1, keepdims=True)
        yn = yc * lax.rsqrt(var + GN_EPS) * lnw_ref[h:h + 1, :] + lnb_ref[h:h + 1, :]
        bonus = jnp.sum(r_ref[0, h] * k_ref[0, h] * rk_ref[h:h + 1, :], axis=-1, keepdims=True) * v_ref[0, h]
        o = ((yn + bonus) * g_ref[0, h]).astype(BF16)
        acc = acc + jnp.dot(o, wo_ref[h * HEAD_DIM:(h + 1) * HEAD_DIM, :], preferred_element_type=F32)
    for h in range(h_moba):
        row0 = (h_rwkv + h) * HEAD_DIM
        acc = acc + jnp.dot(om_ref[0, h].astype(BF16), wo_ref[row0:row0 + HEAD_DIM, :], preferred_element_type=F32)
    o_ref[0] = acc


def mix_out(y, r, k, v, g, o_moba, x, w_out, ln_w, ln_b, r_k, *, tt_target=512):
    b, h_rwkv, t, n = y.shape
    h_moba = o_moba.shape[1]
    d = x.shape[2]
    tt = _row_tile(t, tt_target)
    hm_r = pl.BlockSpec((1, h_rwkv, tt, n), lambda bi, i: (bi, 0, i, 0))
    hm_m = pl.BlockSpec((1, h_moba, tt, n), lambda bi, i: (bi, 0, i, 0))
    full = lambda shape: pl.BlockSpec(shape, lambda bi, i: (0,) * len(shape))
    params = [w_out, ln_w.reshape(h_rwkv, n), ln_b.reshape(h_rwkv, n), r_k.reshape(h_rwkv, n)]
    return pl.pallas_call(
        functools.partial(_mix_out_kernel, h_rwkv=h_rwkv, h_moba=h_moba),
        grid=(b, t // tt),
        in_specs=[hm_r] * 5 + [hm_m, pl.BlockSpec((1, tt, d), lambda bi, i: (bi, i, 0))] + [full(p.shape) for p in params],
        out_specs=pl.BlockSpec((1, tt, d), lambda bi, i: (bi, i, 0)),
        out_shape=jax.ShapeDtypeStruct((b, t, d), F32),
        compiler_params=_cparams("parallel", "parallel"),
        name="mix_out",
    )(y, r, k, v, g, o_moba, x, *params)


def _to_heads(x_rows):
    b, t, dm = x_rows.shape
    return x_rows.reshape(b, t, dm // HEAD_DIM, HEAD_DIM).transpose(0, 2, 1, 3)


def _mixer_front(y, lw, p_prev_rows, wkv_prev, cos, sin):
    b, t, d = y.shape
    flat = y.reshape(b * t, d)
    pr = norm_proj(flat, lw['mix_norm'], lw['w_in_r']).reshape(b, t, -1)
    pm = norm_proj(flat, lw['mix_norm'], lw['w_in_m']).reshape(b, t, -1)
    shift = rms_norm_rows(y[:, -1], lw['mix_norm'])
    r, w, k, v, kk, kka, g = rwkv_prep(pr, p_prev_rows, lw)
    y_wkv, wkv_new = wkv_scan(r, w, k, v, kk, kka, wkv_prev)
    k_rows, q_hm, k_hm, v_hm = rope_qkv(pm, cos, sin)
    d_m = k_rows.shape[2]
    v_rows = pm[:, :, 2 * d_m:]
    return dict(r=r, k=k, v=v, g=g, y=y_wkv, wkv=wkv_new, shift=shift, pm=pm, k_rows=k_rows, v_rows=v_rows,
                q_hm=q_hm, k_hm=k_hm, v_hm=v_hm)


@jax.jit
def kernel(x_prompt, x_sample, cache_k, cache_v, state_wkv, state_shift, page_table, ffn1_norm, ffn1_w_gate,
           ffn1_w_up, ffn1_w_down, mix_norm, w_in, w_out, rwkv_mu, rwkv_w0, rwkv_w2, rwkv_a0, rwkv_a2, rwkv_g2,
           rwkv_k_k, rwkv_k_a, rwkv_r_k, rwkv_ln_w, rwkv_ln_b, ffn2_norm, ffn2_w_gate, ffn2_w_up, ffn2_w_down,
           final_norm):
    b, t, d = x_prompt.shape
    db, ts, _ = x_sample.shape
    depth = w_in.shape[0]
    d_r = rwkv_w0.shape[1]
    h_rwkv = d_r // HEAD_DIM
    c_r = rwkv_mu.shape[1]
    d_m = (w_in.shape[2] - c_r) // 3
    h_moba = d_m // HEAD_DIM
    n_pool, page_size = cache_k.shape[1], cache_k.shape[2]
    past_len = page_table.shape[1] * page_size
    ck = cache_k.reshape(depth, n_pool, page_size, d_m)
    cv = cache_v.reshape(depth, n_pool, page_size, d_m)
    cos_p, sin_p = rope_tables(jnp.arange(t), h_moba)
    cos_s, sin_s = rope_tables(past_len + jnp.arange(ts), h_moba)

    yp, ys = x_prompt, x_sample
    outs = {n: [] for n in ('kp', 'vp', 'ks', 'vs', 'wp', 'ws', 'sp', 'ss')}
    for l in range(depth):
        lw = {'mix_norm': mix_norm[l], 'w_in_r': w_in[l, :, :c_r].astype(BF16), 'w_in_m': w_in[l, :, c_r:].astype(BF16),
              'mu': rwkv_mu[l], 'w0': rwkv_w0[l], 'w2': rwkv_w2[l], 'a0': rwkv_a0[l], 'a2': rwkv_a2[l],
              'g2': rwkv_g2[l], 'k_k': rwkv_k_k[l], 'k_a': rwkv_k_a[l]}
        wo = w_out[l].astype(BF16)
        f1 = (ffn1_norm[l], ffn1_w_gate[l].astype(BF16), ffn1_w_up[l].astype(BF16), ffn1_w_down[l].astype(BF16))
        f2 = (ffn2_norm[l], ffn2_w_gate[l].astype(BF16), ffn2_w_up[l].astype(BF16), ffn2_w_down[l].astype(BF16))

        yp = ffn_half(yp.reshape(b * t, d), *f1).reshape(b, t, d)
        ys = ffn_half(ys.reshape(db * ts, d), *f1).reshape(db, ts, d)

        fp = _mixer_front(yp, lw, jnp.zeros((b, c_r), F32), jnp.zeros((b, h_rwkv, HEAD_DIM, HEAD_DIM), F32),
                          cos_p, sin_p)
        p_prev_s = norm_proj(state_shift[l], lw['mix_norm'], lw['w_in_r'], normalize=False)
        fs = _mixer_front(ys, lw, p_prev_s, state_wkv[l], cos_s, sin_s)

        kmean_p = _to_heads(block_means(fp['k_rows']))
        om_p = moba_prompt(fp['q_hm'], fp['k_hm'], fp['v_hm'], kmean_p)
        kmean_s = sample_block_means(ck, l, page_table)
        q_rows_s = fs['q_hm'].transpose(0, 2, 1, 3).reshape(db, ts, d_m)
        om_s = moba_sample(ck, cv, l, page_table, q_rows_s, fs['k_rows'], fs['v_rows'], kmean_s)

        mix = lambda f, om, y: mix_out(f['y'], f['r'], f['k'], f['v'], f['g'], om, y, wo,
                                       rwkv_ln_w[l], rwkv_ln_b[l], rwkv_r_k[l])
        yp = mix(fp, om_p, yp)
        ys = mix(fs, om_s, ys)

        yp = ffn_half(yp.reshape(b * t, d), *f2).reshape(b, t, d)
        ys = ffn_half(ys.reshape(db * ts, d), *f2).reshape(db, ts, d)

        outs['kp'].append(fp['k_rows'].reshape(b, t, h_moba, HEAD_DIM))
        outs['vp'].append(fp['v_rows'].reshape(b, t, h_moba, HEAD_DIM))
        outs['ks'].append(fs['k_rows'].reshape(db, ts, h_moba, HEAD_DIM))
        outs['vs'].append(fs['v_rows'].reshape(db, ts, h_moba, HEAD_DIM))
        outs['wp'].append(fp['wkv'])
        outs['ws'].append(fs['wkv'])
        outs['sp'].append(fp['shift'])
        outs['ss'].append(fs['shift'])

    y_prompt = rms_norm_rows(yp.reshape(b * t, d), final_norm).reshape(b, t, d)
    y_sample = rms_norm_rows(ys.reshape(db * ts, d), final_norm).reshape(db, ts, d)
    st = lambda n: jnp.stack(outs[n])
    return (y_prompt, y_sample, st('kp'), st('vp'), st('ks'), st('vs'), st('wp'), st('ws'), st('sp'), st('ss'))
```

```python
import functools
import math

import jax
import jax.numpy as jnp
from jax import lax
from jax.experimental import pallas as pl
from jax.experimental.pallas import tpu as pltpu

F32 = jnp.float32
BF16 = jnp.bfloat16
HI = lax.Precision.HIGHEST

HEAD_DIM = 64
MOBA_BLOCK = 256
MOBA_TOPK = 3
ROPE_THETA = 10000.0
NORM_EPS = 1e-6
GN_EPS = 64e-5
NEG_INF = -1e30
DECAY_LORA = 64
ICLR_LORA = 64
GATE_LORA = 128
LANES = 128
WKV_CHUNK = 64

VMEM_LIMIT = 56 * 1024 * 1024


def _cparams(*sem):
    return pltpu.CompilerParams(dimension_semantics=sem, vmem_limit_bytes=VMEM_LIMIT)


def _row_tile(n, target):
    t = min(n, target)
    while n % t:
        t //= 2
    return t


def _nt_dot(a, b, precision=None):
    return lax.dot_general(a, b, (((1,), (1,)), ((), ())), precision=precision, preferred_element_type=F32)


def _ffn_kernel(x_ref, g_ref, wg_ref, wu_ref, wd_ref, o_ref, h_scr, acc_scr):
    j = pl.program_id(1)

    @pl.when(j == 0)
    def _():
        x = x_ref[...]
        ms = jnp.mean(x * x, axis=-1, keepdims=True)
        h_scr[...] = (x * lax.rsqrt(ms + NORM_EPS) * g_ref[...]).astype(BF16)
        acc_scr[...] = jnp.zeros_like(acc_scr)

    h = h_scr[...]
    a = jnp.dot(h, wg_ref[...], preferred_element_type=F32)
    b = jnp.dot(h, wu_ref[...], preferred_element_type=F32)
    u = a * jax.nn.sigmoid(a) * b
    acc_scr[...] += jnp.dot(u.astype(BF16), wd_ref[...], preferred_element_type=F32)

    @pl.when(j == pl.num_programs(1) - 1)
    def _():
        o_ref[...] = x_ref[...] + 0.5 * acc_scr[...]


def ffn_half(x, g, wg, wu, wd, *, tm_target=1024, tf=256):
    n, d = x.shape
    f = wg.shape[1]
    tm = _row_tile(n, tm_target)
    assert f % tf == 0
    return pl.pallas_call(
        _ffn_kernel,
        grid=(n // tm, f // tf),
        in_specs=[
            pl.BlockSpec((tm, d), lambda i, j: (i, 0)),
            pl.BlockSpec((1, d), lambda i, j: (0, 0)),
            pl.BlockSpec((d, tf), lambda i, j: (0, j)),
            pl.BlockSpec((d, tf), lambda i, j: (0, j)),
            pl.BlockSpec((tf, d), lambda i, j: (j, 0)),
        ],
        out_specs=pl.BlockSpec((tm, d), lambda i, j: (i, 0)),
        out_shape=jax.ShapeDtypeStruct((n, d), F32),
        scratch_shapes=[pltpu.VMEM((tm, d), BF16), pltpu.VMEM((tm, d), F32)],
        compiler_params=_cparams("parallel", "arbitrary"),
        name="ffn_half",
    )(x, g.reshape(1, d), wg, wu, wd)


def _proj_kernel(x_ref, g_ref, w_ref, o_ref, h_scr, *, normalize):
    @pl.when(pl.program_id(1) == 0)
    def _():
        x = x_ref[...]
        if normalize:
            ms = jnp.mean(x * x, axis=-1, keepdims=True)
            x = x * lax.rsqrt(ms + NORM_EPS) * g_ref[...]
        h_scr[...] = x.astype(BF16)

    o_ref[...] = jnp.dot(h_scr[...], w_ref[...], preferred_element_type=F32)


def norm_proj(x, g, w, *, normalize=True, tm_target=1024, tn=256):
    n, d = x.shape
    c = w.shape[1]
    tm = _row_tile(n, tm_target)
    assert c % tn == 0
    return pl.pallas_call(
        functools.partial(_proj_kernel, normalize=normalize),
        grid=(n // tm, c // tn),
        in_specs=[
            pl.BlockSpec((tm, d), lambda i, j: (i, 0)),
            pl.BlockSpec((1, d), lambda i, j: (0, 0)),
            pl.BlockSpec((d, tn), lambda i, j: (0, j)),
        ],
        out_specs=pl.BlockSpec((tm, tn), lambda i, j: (i, j)),
        out_shape=jax.ShapeDtypeStruct((n, c), F32),
        scratch_shapes=[pltpu.VMEM((tm, d), BF16)],
        compiler_params=_cparams("parallel", "arbitrary"),
        name="norm_proj",
    )(x, g.reshape(1, d), w)


def _rms_kernel(x_ref, g_ref, o_ref):
    x = x_ref[...]
    ms = jnp.mean(x * x, axis=-1, keepdims=True)
    o_ref[...] = x * lax.rsqrt(ms + NORM_EPS) * g_ref[...]


def rms_norm_rows(x, g, *, tm_target=1024):
    n, d = x.shape
    tm = _row_tile(n, tm_target)
    return pl.pallas_call(
        _rms_kernel,
        grid=(n // tm,),
        in_specs=[pl.BlockSpec((tm, d), lambda i: (i, 0)), pl.BlockSpec((1, d), lambda i: (0, 0))],
        out_specs=pl.BlockSpec((tm, d), lambda i: (i, 0)),
        out_shape=jax.ShapeDtypeStruct((n, d), F32),
        compiler_params=_cparams("parallel"),
        name="rms_norm",
    )(x, g.reshape(1, d))


def _head_block_diag(n):
    r = lax.broadcasted_iota(jnp.int32, (n, n), 0) // HEAD_DIM
    c = lax.broadcasted_iota(jnp.int32, (n, n), 1) // HEAD_DIM
    return (r == c).astype(F32)


def _prep_kernel(*refs, has_prev_tile, with_cum, n_heads):
    if has_prev_tile:
        p_ref, prev8_ref, pprev_ref = refs[:3]
        rest = refs[3:]
    else:
        p_ref, pprev_ref = refs[:2]
        prev8_ref = None
        rest = refs[2:]
    mu_ref, w0_ref, w2_ref, a0_ref, a2_ref, g2_ref, kk_ref, ka_ref = rest[:8]
    r_out, w_out, k_out, v_out, kk_out, kka_out, g_out = rest[8:15]
    c_out = rest[15] if with_cum else None
    d_r = n_heads * HEAD_DIM
    p = p_ref[0]
    tt = p.shape[0]
    prev_row = pprev_ref[0]
    if has_prev_tile:
        prev_row = jnp.where(pl.program_id(1) == 0, prev_row, prev8_ref[0, 7:8, :])
    row = lax.broadcasted_iota(jnp.int32, p.shape, 0)
    if tt % 8 == 0:
        rolled = pltpu.roll(p, 1, 0)
    else:
        rolled = jnp.concatenate([p[tt - 1:], p[:tt - 1]], axis=0)
    p_shift = jnp.where(row == 0, prev_row, rolled)
    xm = p + (p_shift - p) * mu_ref[...]
    r = xm[:, :d_r]
    k = xm[:, d_r:2 * d_r]
    v = xm[:, 2 * d_r:3 * d_r]
    c0 = 3 * d_r
    xw = xm[:, c0:c0 + DECAY_LORA]
    xa = xm[:, c0 + DECAY_LORA:c0 + DECAY_LORA + ICLR_LORA]
    xg = xm[:, c0 + DECAY_LORA + ICLR_LORA:]
    z = -(w0_ref[...] + jnp.dot(jnp.tanh(xw), w2_ref[...], precision=HI, preferred_element_type=F32))
    softplus = jnp.maximum(z, 0.0) + jnp.log(1.0 + jnp.exp(-jnp.abs(z)))
    log_decay = -jnp.exp(-softplus - 0.5)
    if with_cum:
        ci = lax.broadcasted_iota(jnp.int32, (tt, tt), 0)
        cj = lax.broadcasted_iota(jnp.int32, (tt, tt), 1)
        tri = ((ci // WKV_CHUNK == cj // WKV_CHUNK) & (cj <= ci)).astype(F32)
        cum = jnp.dot(tri, log_decay, precision=HI, preferred_element_type=F32)
    a =jax.nn.sigmoid(a0_ref[...] + jnp.dot(xa, a2_ref[...], precision=HI, preferred_element_type=F32))
    g = jnp.dot(jax.nn.sigmoid(xg), g2_ref[...], precision=HI, preferred_element_type=F32)
    kk = k * kk_ref[...]
    ss = jnp.dot(kk * kk, _head_block_diag(d_r), precision=HI, preferred_element_type=F32)
    kk = kk / jnp.maximum(jnp.sqrt(ss), 1e-12)
    k2 = k * (1.0 + (a - 1.0) * ka_ref[...])
    kka = kk * a
    for h in range(n_heads):
        sl = slice(h * HEAD_DIM, (h + 1) * HEAD_DIM)
        r_out[0, h] = r[:, sl]
        w_out[0, h] = log_decay[:, sl]
        if with_cum:
            c_out[0, h] = cum[:, sl]
        k_out[0, h] = k2[:, sl]
        v_out[0, h] = v[:, sl]
        kk_out[0, h] = kk[:, sl]
        kka_out[0, h] = kka[:, sl]
        g_out[0, h] = g[:, sl]


def rwkv_prep(pr, p_prev, lp, *, tt_target=256):
    b, t, c = pr.shape
    d_r = lp['w0'].shape[0]
    n_heads = d_r // HEAD_DIM
    tt = _row_tile(t, tt_target)
    nt = t // tt
    has_prev = nt > 1
    with_cum = tt % WKV_CHUNK == 0
    n_out = 8 if with_cum else 7
    row = lambda x: x.reshape(1, -1)
    full = lambda shape: pl.BlockSpec(shape, lambda bi, i: (0,) * len(shape))
    in_specs = [pl.BlockSpec((1, tt, c), lambda bi, i: (bi, i, 0))]
    args = [pr]
    if has_prev:
        in_specs.append(pl.BlockSpec((1, 8, c), lambda bi, i: (bi, jnp.maximum(i * (tt // 8) - 1, 0), 0)))
        args.append(pr)
    in_specs.append(pl.BlockSpec((1, 1, c), lambda bi, i: (bi, 0, 0)))
    args.append(p_prev.reshape(b, 1, c))
    params = [row(lp['mu']), row(lp['w0']), lp['w2'], row(lp['a0']), lp['a2'], lp['g2'], row(lp['k_k']), row(lp['k_a'])]
    in_specs += [full(x.shape) for x in params]
    args += params
    hm = jax.ShapeDtypeStruct((b, n_heads, t, HEAD_DIM), F32)
    hm_spec = pl.BlockSpec((1, n_heads, tt, HEAD_DIM), lambda bi, i: (bi, 0, i, 0))
    return pl.pallas_call(
        functools.partial(_prep_kernel, has_prev_tile=has_prev, with_cum=with_cum, n_heads=n_heads),
        grid=(b, nt),
        in_specs=in_specs,
        out_specs=[hm_spec] * n_out,
        out_shape=[hm] * n_out,
        compiler_params=_cparams("parallel", "arbitrary"),
        name="rwkv_prep",
    )(*args)


def _scan_kernel(r_ref, lw_ref, k_ref, v_ref, kk_ref, kka_ref, s0_ref, y_ref, s_out_ref, s_scr, *, n_heads, tc):
    @pl.when(pl.program_id(1) == 0)
    def _():
        s_scr[...] = s0_ref[0]

    eye = (lax.broadcasted_iota(jnp.int32, (HEAD_DIM, HEAD_DIM), 0)
           == lax.broadcasted_iota(jnp.int32, (HEAD_DIM, HEAD_DIM), 1))

    def body(t, carry):
        for h in range(n_heads):
            s = s_scr[h]
            row = lambda ref: ref[0, h, pl.ds(t, 1), :]
            sa = -jnp.sum(s * row(kk_ref), axis=1, keepdims=True)
            v_col = jnp.sum(jnp.where(eye, row(v_ref), 0.0), axis=1, keepdims=True)
            s = s * jnp.exp(row(lw_ref)) + sa * row(kka_ref) + v_col * row(k_ref)
            s_scr[h] = s
            y_col = jnp.sum(s * row(r_ref), axis=1, keepdims=True)
            y_ref[0, h, pl.ds(t, 1), :] = jnp.sum(jnp.where(eye, y_col, 0.0), axis=0, keepdims=True)
        return carry

    lax.fori_loop(0, tc, body, 0)
    s_out_ref[0] = s_scr[...]


def _bdot(a, b, contract, precision=None):
    if precision is None:
        a, b = a.astype(BF16), b.astype(BF16)
    return lax.dot_general(a, b, (((contract[0],), (contract[1],)), ((0,), (0,))), precision=precision,
                           preferred_element_type=F32)


_NN, _NT, _TN = (2, 1), (2, 2), (1, 1)


def _unit_lower_inverse(t_strict):
    n = t_strict.shape[1]
    ri = lax.broadcasted_iota(jnp.int32, (n, n), 0)
    ci = lax.broadcasted_iota(jnp.int32, (n, n), 1)
    same = lambda size: (ri // size) == (ci // size)
    mm = lambda x, y: _bdot(x, y, _NN, precision=HI)
    neg = jnp.where(same(8), -t_strict, 0.0)
    n2 = mm(neg, neg)
    n4 = mm(n2, n2)
    inv = (ri == ci).astype(F32) + neg
    inv = inv + mm(inv, n2)
    inv = inv + mm(inv, n4)
    for size in (16, 32, 64):
        off = jnp.where(same(size) & jnp.logical_not(same(size // 2)), t_strict, 0.0)
        inv = inv - mm(inv, mm(off, inv))
    return inv


def _chunk_scan_kernel(r_ref, lw_ref, c_ref, k_ref, v_ref, kk_ref, kka_ref, s0_ref, y_ref, s_out_ref, s_scr,
                       *, n_heads, n_chunks):
    L = WKV_CHUNK

    @pl.when(pl.program_id(1) == 0)
    def _():
        s_scr[...] = s0_ref[0]

    ri = lax.broadcasted_iota(jnp.int32, (2 * L, 2 * L), 0)
    ci = lax.broadcasted_iota(jnp.int32, (2 * L, 2 * L), 1)
    keep = ((ri < L) & ((ci % L) < ri)) | ((ri >= L) & ((ci % L) <= (ri - L)))

    def body(cidx, carry):
        rows = pl.ds(pl.multiple_of(cidx * L, L), L)
        ld = lambda ref: ref[0, :, rows, :]
        r, lw, c, k, v, a, b = (ld(x) for x in (r_ref, lw_ref, c_ref, k_ref, v_ref, kk_ref, kka_ref))
        c_last = c[:, L - 1:L, :]
        inv_g = jnp.exp(-c)
        a_t = a * jnp.exp(c - lw)
        b_t = b * inv_g
        k_t = k * inv_g
        r_t = r * jnp.exp(c)
        to_end = jnp.exp(c_last - c)
        b_end = b * to_end
        k_end = k * to_end
        coef = _bdot(jnp.concatenate([a_t, r_t], axis=1), jnp.concatenate([b_t, k_t], axis=1), _NT,
                     precision=HI)
        coef = jnp.where(keep, coef, 0.0)
        t_ab, t_ak = coef[:, :L, :L], coef[:, :L, L:]
        a_rb, a_rk = coef[:, L:, :L], coef[:, L:, L:]
        m_inv = _unit_lower_inverse(t_ab)
        w_mat = _bdot(m_inv, a_t, _NN)
        u_mat = -_bdot(m_inv, _bdot(t_ak, v, _NN), _NN)
        y_intra = _bdot(a_rb, u_mat, _NN) + _bdot(a_rk, v, _NN)
        r_w = r_t - _bdot(a_rb, w_mat, _NN)
        h_mat = _bdot(u_mat, b_end, _TN) + _bdot(v, k_end, _TN)
        s0 = s_scr[...]
        y_ref[0, :, rows, :] = _bdot(r_w, s0, _NT) + y_intra
        p = _bdot(s0, w_mat, _NT)
        s_scr[...] = s0 * jnp.exp(c_last) - _bdot(p, b_end, _NN) + h_mat
        return carry

    lax.fori_loop(0, n_chunks, body, 0)
    s_out_ref[0] = s_scr[...]


def wkv_chunk_scan(r, lw, c, k, v, kk, kka, s0, *, tc_target=256):
    b, h, t, n = r.shape
    tc = _row_tile(t, tc_target)
    assert tc % WKV_CHUNK == 0 and n == WKV_CHUNK
    seq = pl.BlockSpec((1, h, tc, n), lambda bi, i: (bi, 0, i, 0))
    st = pl.BlockSpec((1, h, n, n), lambda bi, i: (bi, 0, 0, 0))
    return pl.pallas_call(
        functools.partial(_chunk_scan_kernel, n_heads=h, n_chunks=tc // WKV_CHUNK),
        grid=(b, t // tc),
        in_specs=[seq] * 7 + [st],
        out_specs=[seq, st],
        out_shape=[jax.ShapeDtypeStruct((b, h, t, n), F32), jax.ShapeDtypeStruct((b, h, n, n), F32)],
        scratch_shapes=[pltpu.VMEM((h, n, n), F32)],
        compiler_params=_cparams("parallel", "arbitrary"),
        name="wkv_chunk_scan",
    )(r, lw, c, k, v, kk, kka, s0)


def wkv_scan(r, w, k, v, kk, kka, s0, *, tc_target=256):
    b, h, t, n = r.shape
    tc = _row_tile(t, tc_target)
    seq = pl.BlockSpec((1, h, tc, n), lambda bi, c: (bi, 0, c, 0))
    st = pl.BlockSpec((1, h, n, n), lambda bi, c: (bi, 0, 0, 0))
    return pl.pallas_call(
        functools.partial(_scan_kernel, n_heads=h, tc=tc),
        grid=(b, t // tc),
        in_specs=[seq] * 6 + [st],
        out_specs=[seq, st],
        out_shape=[jax.ShapeDtypeStruct((b, h, t, n), F32), jax.ShapeDtypeStruct((b, h, n, n), F32)],
        scratch_shapes=[pltpu.VMEM((h, n, n), F32)],
        compiler_params=_cparams("parallel", "arbitrary"),
        name="wkv_scan",
    )(r, w, k, v, kk, kka, s0)


def _rope_kernel(p_ref, cos_ref, sin_ref, krow_ref, q_out, k_out=None, vt_out=None, *, n_heads, head_major):
    d_m = n_heads * HEAD_DIM
    half = HEAD_DIM // 2
    cos = cos_ref[...]
    sin = sin_ref[...]
    first_half = (lax.broadcasted_iota(jnp.int32, cos.shape, 1) % HEAD_DIM) < half

    def rot(x):
        if x.shape[0] % 8 == 0:
            fwd = pltpu.roll(x, d_m - half, 1)
            bwd = pltpu.roll(x, half, 1)
        else:
            fwd = jnp.concatenate([x[:, half:], x[:, :half]], axis=1)
            bwd = jnp.concatenate([x[:, d_m - half:], x[:, :d_m - half]], axis=1)
        return x * cos + jnp.where(first_half, fwd, bwd) * sin

    p = p_ref[0]
    q = rot(p[:, :d_m])
    k = rot(p[:, d_m:2 * d_m])
    krow_ref[0] = k
    if not head_major:
        q_out[0] = q
        return
    v = p[:, 2 * d_m:]
    for h in range(n_heads):
        sl = slice(h * HEAD_DIM, (h + 1) * HEAD_DIM)
        q_out[0, h] = q[:, sl]
        k_out[0, h] = k[:, sl].astype(BF16)
        for jb in range(p.shape[0] // MOBA_BLOCK):
            vt_out[0, h, jb] = v[jb * MOBA_BLOCK:(jb + 1) * MOBA_BLOCK, sl].T.astype(BF16)


def rope_qkv(pm, cos, sin, *, head_major, tt_target=512):
    b, t, c = pm.shape
    d_m = c // 3
    n_heads = d_m // HEAD_DIM
    tt = _row_tile(t, tt_target)
    rows_spec = pl.BlockSpec((1, tt, d_m), lambda bi, i: (bi, i, 0))
    rows = jax.ShapeDtypeStruct((b, t, d_m), F32)
    if head_major:
        assert tt % MOBA_BLOCK == 0
        hm_spec = pl.BlockSpec((1, n_heads, tt, HEAD_DIM), lambda bi, i: (bi, 0, i, 0))
        hm = lambda dt: jax.ShapeDtypeStruct((b, n_heads, t, HEAD_DIM), dt)
        nbt = tt // MOBA_BLOCK
        out_specs = [rows_spec, hm_spec, hm_spec,
                     pl.BlockSpec((1, n_heads, nbt, HEAD_DIM, MOBA_BLOCK), lambda bi, i: (bi, 0, i, 0, 0))]
        out_shape = [rows, hm(F32), hm(BF16),
                     jax.ShapeDtypeStruct((b, n_heads, t // MOBA_BLOCK, HEAD_DIM, MOBA_BLOCK), BF16)]
    else:
        out_specs = [rows_spec, rows_spec]
        out_shape = [rows, rows]
    return pl.pallas_call(
        functools.partial(_rope_kernel, n_heads=n_heads, head_major=head_major),
        grid=(b, t // tt),
        in_specs=[
            pl.BlockSpec((1, tt, c), lambda bi, i: (bi, i, 0)),
            pl.BlockSpec((tt, d_m), lambda bi, i: (i, 0)),
            pl.BlockSpec((tt, d_m), lambda bi, i: (i, 0)),
        ],
        out_specs=out_specs,
        out_shape=out_shape,
        compiler_params=_cparams("parallel", "parallel"),
        name="rope_qkv",
    )(pm, cos, sin)


def rope_tables(pos, n_heads):
    half = HEAD_DIM // 2
    inv_freq = ROPE_THETA ** (-jnp.arange(half, dtype=F32) / half)
    ang = pos.astype(F32)[:, None] * inv_freq[None, :]
    cos = jnp.cos(ang)
    sin = jnp.sin(ang)
    cos = jnp.tile(jnp.concatenate([cos, cos], axis=1), (1, n_heads))
    sin = jnp.tile(jnp.concatenate([-sin, sin], axis=1), (1, n_heads))
    return cos, sin


def _kmean_kernel(k_ref, o_ref, *, nblk):
    for j in range(nblk):
        o_ref[0, j:j + 1, :] = jnp.mean(k_ref[0, j * MOBA_BLOCK:(j + 1) * MOBA_BLOCK, :], axis=0, keepdims=True)


def block_means(k_rows):
    b, t, d_m = k_rows.shape
    nb = t // MOBA_BLOCK
    nblk = 8 if nb % 8 == 0 else nb
    return pl.pallas_call(
        functools.partial(_kmean_kernel, nblk=nblk),
        grid=(b, nb // nblk),
        in_specs=[pl.BlockSpec((1, nblk * MOBA_BLOCK, d_m), lambda bi, i: (bi, i, 0))],
        out_specs=pl.BlockSpec((1, nblk, d_m), lambda bi, i: (bi, i, 0)),
        out_shape=jax.ShapeDtypeStruct((b, nb, d_m), F32),
        compiler_params=_cparams("parallel", "parallel"),
        name="block_means",
    )(k_rows)


def _top_blocks(gate, valid, n_sel):
    nb = gate.shape[1]
    blk = lax.broadcasted_iota(jnp.int32, gate.shape, 1)
    avail = valid
    sel = jnp.zeros(gate.shape, F32)
    for _ in range(n_sel):
        g = jnp.where(avail, gate, -jnp.inf)
        m = jnp.max(g, axis=1, keepdims=True)
        first = jnp.min(jnp.where((g == m) & avail, blk, nb), axis=1, keepdims=True)
        pick = blk == first
        sel = jnp.where(pick, 1.0, sel)
        avail = avail & jnp.logical_not(pick)
    return sel


def _top_blocks_t(gate, valid, n_sel):
    ax = gate.ndim - 2
    nb = gate.shape[ax]
    blk = lax.broadcasted_iota(jnp.int32, gate.shape, ax)
    avail = jnp.broadcast_to(valid, gate.shape)
    sel = jnp.zeros(gate.shape, F32)
    for _ in range(n_sel):
        g = jnp.where(avail, gate, -jnp.inf)
        m = jnp.max(g, axis=ax, keepdims=True)
        first = jnp.min(jnp.where((g == m) & avail, blk, nb), axis=ax, keepdims=True)
        pick = blk == first
        sel = jnp.where(pick, 1.0, sel)
        avail = avail & jnp.logical_not(pick)
    return sel


MOBA_HEADS_PER_STEP = 2
SUBLANES = 8
MOBA_BLOCKS_PER_ITER = SUBLANES // 2


def _moba_prompt_kernel(q_ref, k_ref, vt_ref, km_ref, o_ref, sel_scr, *, n_sel):
    i = pl.program_id(2)
    hb = q_ref.shape[1]
    nb = km_ref.shape[2]
    q32 = q_ref[0]
    q = (q32 * HEAD_DIM ** -0.5).astype(BF16)

    def scores_t(j0, n_blocks):
        rows = pl.ds(pl.multiple_of(j0 * MOBA_BLOCK, MOBA_BLOCK), n_blocks * MOBA_BLOCK)
        return _bdot(k_ref[0, :, rows, :], q, _NT)

    def weighted_values_t(j0, p):
        out = None
        for jb in range(p.shape[1] // MOBA_BLOCK):
            pv = _bdot(vt_ref[0, :, j0 + jb], p[:, jb * MOBA_BLOCK:(jb + 1) * MOBA_BLOCK], _NN)
            out = pv if out is None else out + pv
        return out

    key_pos = lax.broadcasted_iota(jnp.int32, (MOBA_BLOCK, MOBA_BLOCK), 0)
    q_pos = lax.broadcasted_iota(jnp.int32, (MOBA_BLOCK, MOBA_BLOCK), 1)
    s = jnp.where(key_pos <= q_pos, scores_t(i, 1), NEG_INF)
    m = jnp.max(s, axis=1, keepdims=True)
    p = jnp.exp(s - m)
    l = jnp.sum(p, axis=1, keepdims=True)
    acc = weighted_values_t(i, p)

    if n_sel > 0:
        gate_t = _bdot(km_ref[0], q32, _NT, precision=HI)
        blk = lax.broadcasted_iota(jnp.int32, (nb, MOBA_BLOCK), 0)
        sel_scr[...] = _top_blocks_t(gate_t, blk < i, n_sel)
        kb = MOBA_BLOCKS_PER_ITER

        def body(jj, carry):
            m, l, acc = carry
            j0 = jj * kb
            sel8 = sel_scr[:, pl.ds(pl.multiple_of((jj // 2) * 8, 8), 8), :]
            chosen = jnp.where(jj % 2 == 0, sel8[:, :kb], sel8[:, kb:]) > 0.0
            s = scores_t(j0, kb).reshape(hb, kb, MOBA_BLOCK, MOBA_BLOCK)
            s = jnp.where(chosen[:, :, None, :], s, NEG_INF).reshape(hb, kb * MOBA_BLOCK, MOBA_BLOCK)
            m_new = jnp.maximum(m, jnp.max(s, axis=1, keepdims=True))
            alpha = jnp.exp(m - m_new)
            p = jnp.exp(s - m_new)
            l = alpha * l + jnp.sum(p, axis=1, keepdims=True)
            acc = alpha * acc + weighted_values_t(j0, p)
            return m_new, l, acc

        m, l, acc = lax.fori_loop(0, (i + kb - 1) // kb, body, (m, l, acc))

    o_t = acc / l
    for h in range(hb):
        o_ref[0, h] = o_t[h].T


def moba_prompt(q, k, vt, kmean):
    b, h, t, d = q.shape
    assert t % MOBA_BLOCK == 0
    nb = t // MOBA_BLOCK
    n_sel = min(MOBA_TOPK, nb - 1)
    nb_pad = -(-nb // 8) * 8
    kmean = jnp.pad(kmean, ((0, 0), (0, 0), (0, nb_pad - nb), (0, 0)))
    hb = MOBA_HEADS_PER_STEP
    assert h % hb == 0 and nb % MOBA_BLOCKS_PER_ITER == 0
    q_spec = pl.BlockSpec((1, hb, MOBA_BLOCK, d), lambda bi, hi, i: (bi, hi, i, 0))
    return pl.pallas_call(
        functools.partial(_moba_prompt_kernel, n_sel=n_sel),
        grid=(b, h // hb, nb),
        in_specs=[
            q_spec,
            pl.BlockSpec((1, hb, t, d), lambda bi, hi, i: (bi, hi, 0, 0)),
            pl.BlockSpec((1, hb, nb, d, MOBA_BLOCK), lambda bi, hi, i: (bi, hi, 0, 0, 0)),
            pl.BlockSpec((1, hb, nb_pad, d), lambda bi, hi, i: (bi, hi, 0, 0)),
        ],
        out_specs=q_spec,
        out_shape=jax.ShapeDtypeStruct((b, h, t, d), F32),
        scratch_shapes=[pltpu.VMEM((hb, nb_pad, MOBA_BLOCK), F32)],
        compiler_params=_cparams("parallel", "parallel", "arbitrary"),
        name="moba_prompt",
    )(q, k, vt, kmean)


PAGES_PER_STEP = 16


def _sample_kmean_kernel(pt_ref, *refs, ppb, page_size):
    pages, o_ref = refs[:-1], refs[-1]
    nblk = len(pages) // ppb
    for j in range(nblk):
        s = jnp.sum(pages[j * ppb][0, 0], axis=0, keepdims=True)
        for q in range(1, ppb):
            s = s + jnp.sum(pages[j * ppb + q][0, 0], axis=0, keepdims=True)
        o_ref[0, j:j + 1, :] = s / (ppb * page_size)


def sample_block_means(cache_k, layer, page_table):
    _, _, page_size, d_m = cache_k.shape
    db, n_pages = page_table.shape
    ppb = MOBA_BLOCK // page_size
    n_full = (n_pages * page_size) // MOBA_BLOCK
    pps = PAGES_PER_STEP
    assert n_pages % pps == 0 and (pps // ppb) % 8 == 0

    def page_spec(q):
        return pl.BlockSpec((1, 1, page_size, d_m), lambda bi, g, pt: (layer, pt[bi, g * pps + q], 0, 0))

    return pl.pallas_call(
        functools.partial(_sample_kmean_kernel, ppb=ppb, page_size=page_size),
        grid_spec=pltpu.PrefetchScalarGridSpec(
            num_scalar_prefetch=1,
            grid=(db, n_pages // pps),
            in_specs=[page_spec(q) for q in range(pps)],
            out_specs=pl.BlockSpec((1, pps // ppb, d_m), lambda bi, g, pt: (bi, g, 0)),
        ),
        out_shape=jax.ShapeDtypeStruct((db, n_full, d_m), F32),
        compiler_params=_cparams("parallel", "arbitrary"),
        name="sample_block_means",
    )(page_table, *([cache_k] * pps))


def _moba_sample_kernel(pt_ref, q_ref, kn_ref, vn_ref, km_ref, *refs, ppb, n_heads, n_sel):
    k_pages = refs[:ppb]
    v_pages = refs[ppb:2 * ppb]
    o_ref = refs[2 * ppb]
    qbd_scr, sel_scr, m_scr, l_scr, acc_scr = refs[2 * ppb + 1:]
    n = pl.program_id(1)
    rows = q_ref.shape[1]
    t = rows // n_heads
    d_m = n_heads * HEAD_DIM
    nb = km_ref.shape[1]
    n_full = pl.num_programs(1)
    scale = HEAD_DIM ** -0.5
    row_head = lax.broadcasted_iota(jnp.int32, (rows, d_m), 0) // t
    col_head = lax.broadcasted_iota(jnp.int32, (rows, d_m), 1) // HEAD_DIM
    own_head = row_head == col_head

    @pl.when(n == 0)
    def _():
        qbd = jnp.where(own_head, q_ref[0], 0.0)
        qbd_scr[...] = qbd
        if n_sel > 0:
            gate = _nt_dot(qbd, km_ref[0], precision=HI)
            blk = lax.broadcasted_iota(jnp.int32, gate.shape, 1)
            sel_scr[...] = _top_blocks(gate, blk < n_full, n_sel)
        s = _nt_dot(qbd.astype(BF16), kn_ref[0].astype(BF16)) * scale
        q_t = lax.broadcasted_iota(jnp.int32, s.shape, 0) % t
        k_t = lax.broadcasted_iota(jnp.int32, s.shape, 1)
        s = jnp.where(k_t <= q_t, s, NEG_INF)
        m = jnp.max(s, axis=1, keepdims=True)
        p = jnp.exp(s - m)
        m_scr[...] = m
        l_scr[...] = jnp.sum(p, axis=1, keepdims=True)
        acc_scr[...] = jnp.dot(p.astype(BF16), vn_ref[0].astype(BF16), preferred_element_type=F32)

    if n_sel > 0:
        blk = lax.broadcasted_iota(jnp.int32, (rows, nb), 1)
        col = jnp.sum(jnp.where(blk == n, sel_scr[...], 0.0), axis=1, keepdims=True)
        qb = qbd_scr[...].astype(BF16)
        for pg in range(ppb):
            s = jnp.where(col > 0.0, _nt_dot(qb, k_pages[pg][0, 0].astype(BF16)) * scale, NEG_INF)
            m = m_scr[...]
            m_new = jnp.maximum(m, jnp.max(s, axis=1, keepdims=True))
            alpha = jnp.exp(m - m_new)
            p = jnp.exp(s - m_new)
            m_scr[...] = m_new
            l_scr[...] = alpha * l_scr[...] + jnp.sum(p, axis=1, keepdims=True)
            acc_scr[...] = alpha * acc_scr[...] + jnp.dot(p.astype(BF16), v_pages[pg][0, 0].astype(BF16),
                                                           preferred_element_type=F32)

    @pl.when(n == pl.num_programs(1) - 1)
    def _():
        o_ref[0] = acc_scr[...] / l_scr[...]


def moba_sample(cache_k, cache_v, layer, page_table, q_rows, k_rows, v_rows, kmean):
    _, _, page_size, d_m = cache_k.shape
    db, t, _ = q_rows.shape
    n_pages = page_table.shape[1]
    n_heads = d_m // HEAD_DIM
    ppb = MOBA_BLOCK // page_size
    n_full = (n_pages * page_size) // MOBA_BLOCK
    assert n_full * ppb == n_pages, "past rows inside the current block are not supported"
    n_sel = min(MOBA_TOPK, n_full)
    assert n_sel > 0 and t <= page_size
    rows = n_heads * t
    nb_pad = -(-n_full // LANES) * LANES
    q_rep = jnp.tile(q_rows, (1, n_heads, 1))
    pad_rows = lambda x: jnp.pad(x, ((0, 0), (0, page_size - t), (0, 0)))
    km_pad = jnp.pad(kmean, ((0, 0), (0, nb_pad - n_full), (0, 0)))
    q_spec = pl.BlockSpec((1, rows, d_m), lambda bi, n, pt: (bi, 0, 0))
    new_spec = pl.BlockSpec((1, page_size, d_m), lambda bi, n, pt: (bi, 0, 0))

    def page_spec(q):
        return pl.BlockSpec((1, 1, page_size, d_m), lambda bi, n, pt: (layer, pt[bi, n * ppb + q], 0, 0))

    o = pl.pallas_call(
        functools.partial(_moba_sample_kernel, ppb=ppb, n_heads=n_heads, n_sel=n_sel),
        grid_spec=pltpu.PrefetchScalarGridSpec(
            num_scalar_prefetch=1,
            grid=(db, n_full),
            in_specs=[q_spec, new_spec, new_spec, pl.BlockSpec((1, nb_pad, d_m), lambda bi, n, pt: (bi, 0, 0))]
            + [page_spec(q) for q in range(ppb)] * 2,
            out_specs=q_spec,
            scratch_shapes=[
                pltpu.VMEM((rows, d_m), F32),
                pltpu.VMEM((rows, nb_pad), F32),
                pltpu.VMEM((rows, 1), F32),
                pltpu.VMEM((rows, 1), F32),
                pltpu.VMEM((rows, d_m), F32),
            ],
        ),
        out_shape=jax.ShapeDtypeStruct((db, rows, d_m), F32),
        compiler_params=_cparams("parallel", "arbitrary"),
        name="moba_sample",
    )(page_table, q_rep, pad_rows(k_rows), pad_rows(v_rows), km_pad, *([cache_k] * ppb), *([cache_v] * ppb))
    o = o.reshape(db, n_heads, t, n_heads, HEAD_DIM)
    return jnp.stack([o[:, h, :, h, :] for h in range(n_heads)], axis=1)


def _mix_out_kernel(y_ref, r_ref, k_ref, v_ref, g_ref, om_ref, x_ref, wo_ref, lnw_ref, lnb_ref, rk_ref, o_ref,
                    *, h_rwkv, h_moba):
    acc = x_ref[0]
    for h in range(h_rwkv):
        y = y_ref[0, h]
        mean = jnp.mean(y, axis=-1, keepdims=True)
        yc = y - mean
        var = jnp.mean(yc * yc, axis=-1, keepdims=True)
        yn = yc * lax.rsqrt(var + GN_EPS) * lnw_ref[h:h + 1, :] + lnb_ref[h:h + 1, :]
        bonus = jnp.sum(r_ref[0, h] * k_ref[0, h] * rk_ref[h:h + 1, :], axis=-1, keepdims=True) * v_ref[0, h]
        o = ((yn + bonus) * g_ref[0, h]).astype(BF16)
        acc = acc + jnp.dot(o, wo_ref[h * HEAD_DIM:(h + 1) * HEAD_DIM, :], preferred_element_type=F32)
    for h in range(h_moba):
        row0 = (h_rwkv + h) * HEAD_DIM
        acc = acc + jnp.dot(om_ref[0, h].astype(BF16), wo_ref[row0:row0 + HEAD_DIM, :], preferred_element_type=F32)
    o_ref[0] = acc


def mix_out(y, r, k, v, g, o_moba, x, w_out, ln_w, ln_b, r_k, *, tt_target=512):
    b, h_rwkv, t, n = y.shape
    h_moba = o_moba.shape[1]
    d = x.shape[2]
    tt = _row_tile(t, tt_target)
    hm_r = pl.BlockSpec((1, h_rwkv, tt, n), lambda bi, i: (bi, 0, i, 0))
    hm_m = pl.BlockSpec((1, h_moba, tt, n), lambda bi, i: (bi, 0, i, 0))
    full = lambda shape: pl.BlockSpec(shape, lambda bi, i: (0,) * len(shape))
    params = [w_out, ln_w.reshape(h_rwkv, n), ln_b.reshape(h_rwkv, n), r_k.reshape(h_rwkv, n)]
    return pl.pallas_call(
        functools.partial(_mix_out_kernel, h_rwkv=h_rwkv, h_moba=h_moba),
        grid=(b, t // tt),
        in_specs=[hm_r] * 5 + [hm_m, pl.BlockSpec((1, tt, d), lambda bi, i: (bi, i, 0))] + [full(p.shape) for p in params],
        out_specs=pl.BlockSpec((1, tt, d), lambda bi, i: (bi, i, 0)),
        out_shape=jax.ShapeDtypeStruct((b, t, d), F32),
        compiler_params=_cparams("parallel", "parallel"),
        name="mix_out",
    )(y, r, k, v, g, o_moba, x, *params)


def _to_heads(x_rows):
    b, t, dm = x_rows.shape
    return x_rows.reshape(b, t, dm // HEAD_DIM, HEAD_DIM).transpose(0, 2, 1, 3)


def _mixer_front(y, lw, p_prev_rows, wkv_prev, cos, sin, *, head_major):
    b, t, d = y.shape
    flat = y.reshape(b * t, d)
    pr = norm_proj(flat, lw['mix_norm'], lw['w_in_r']).reshape(b, t, -1)
    pm = norm_proj(flat, lw['mix_norm'], lw['w_in_m']).reshape(b, t, -1)
    shift = rms_norm_rows(y[:, -1], lw['mix_norm'])
    r, log_w, k, v, kk, kka, g, *cum = rwkv_prep(pr, p_prev_rows, lw)
    if cum:
        y_wkv, wkv_new = wkv_chunk_scan(r, log_w, cum[0], k, v, kk, kka, wkv_prev)
    else:
        y_wkv, wkv_new = wkv_scan(r, log_w, k, v, kk, kka, wkv_prev)
    k_rows, *moba_in = rope_qkv(pm, cos, sin, head_major=head_major)
    d_m = k_rows.shape[2]
    v_rows = pm[:, :, 2 * d_m:]
    return dict(r=r, k=k, v=v, g=g, y=y_wkv, wkv=wkv_new, shift=shift, k_rows=k_rows, v_rows=v_rows, moba_in=moba_in)


@jax.jit
def kernel(x_prompt, x_sample, cache_k, cache_v, state_wkv, state_shift, page_table, ffn1_norm, ffn1_w_gate,
           ffn1_w_up, ffn1_w_down, mix_norm, w_in, w_out, rwkv_mu, rwkv_w0, rwkv_w2, rwkv_a0, rwkv_a2, rwkv_g2,
           rwkv_k_k, rwkv_k_a, rwkv_r_k, rwkv_ln_w, rwkv_ln_b, ffn2_norm, ffn2_w_gate, ffn2_w_up, ffn2_w_down,
           final_norm):
    b, t, d = x_prompt.shape
    db, ts, _ = x_sample.shape
    depth = w_in.shape[0]
    d_r = rwkv_w0.shape[1]
    h_rwkv = d_r // HEAD_DIM
    c_r = rwkv_mu.shape[1]
    d_m = (w_in.shape[2] - c_r) // 3
    h_moba = d_m // HEAD_DIM
    n_pool, page_size = cache_k.shape[1], cache_k.shape[2]
    past_len = page_table.shape[1] * page_size
    ck = cache_k.reshape(depth, n_pool, page_size, d_m)
    cv = cache_v.reshape(depth, n_pool, page_size, d_m)
    cos_p, sin_p = rope_tables(jnp.arange(t), h_moba)
    cos_s, sin_s = rope_tables(past_len + jnp.arange(ts), h_moba)

    yp, ys = x_prompt, x_sample
    outs = {n: [] for n in ('kp', 'vp', 'ks', 'vs', 'wp', 'ws', 'sp', 'ss')}
    for l in range(depth):
        lw = {'mix_norm': mix_norm[l], 'w_in_r': w_in[l, :, :c_r].astype(BF16), 'w_in_m': w_in[l, :, c_r:].astype(BF16),
              'mu': rwkv_mu[l], 'w0': rwkv_w0[l], 'w2': rwkv_w2[l], 'a0': rwkv_a0[l], 'a2': rwkv_a2[l],
              'g2': rwkv_g2[l], 'k_k': rwkv_k_k[l], 'k_a': rwkv_k_a[l]}
        wo = w_out[l].astype(BF16)
        f1 = (ffn1_norm[l], ffn1_w_gate[l].astype(BF16), ffn1_w_up[l].astype(BF16), ffn1_w_down[l].astype(BF16))
        f2 = (ffn2_norm[l], ffn2_w_gate[l].astype(BF16), ffn2_w_up[l].astype(BF16), ffn2_w_down[l].astype(BF16))

        yp = ffn_half(yp.reshape(b * t, d), *f1).reshape(b, t, d)
        ys = ffn_half(ys.reshape(db * ts, d), *f1).reshape(db, ts, d)

        fp = _mixer_front(yp, lw, jnp.zeros((b, c_r), F32), jnp.zeros((b, h_rwkv, HEAD_DIM, HEAD_DIM), F32),
                          cos_p, sin_p, head_major=True)
        p_prev_s = norm_proj(state_shift[l], lw['mix_norm'], lw['w_in_r'], normalize=False)
        fs = _mixer_front(ys, lw, p_prev_s, state_wkv[l], cos_s, sin_s, head_major=False)

        kmean_p = _to_heads(block_means(fp['k_rows']))
        om_p = moba_prompt(*fp['moba_in'], kmean_p)
        kmean_s = sample_block_means(ck, l, page_table)
        om_s = moba_sample(ck, cv, l, page_table, fs['moba_in'][0], fs['k_rows'], fs['v_rows'], kmean_s)

        mix = lambda f, om, y: mix_out(f['y'], f['r'], f['k'], f['v'], f['g'], om, y, wo,
                                       rwkv_ln_w[l], rwkv_ln_b[l], rwkv_r_k[l])
        yp = mix(fp, om_p, yp)
        ys = mix(fs, om_s, ys)

        yp = ffn_half(yp.reshape(b * t, d), *f2).reshape(b, t, d)
        ys = ffn_half(ys.reshape(db * ts, d), *f2).reshape(db, ts, d)

        outs['kp'].append(fp['k_rows'].reshape(b, t, h_moba, HEAD_DIM))
        outs['vp'].append(fp['v_rows'].reshape(b, t, h_moba, HEAD_DIM))
        outs['ks'].append(fs['k_rows'].reshape(db, ts, h_moba, HEAD_DIM))
        outs['vs'].append(fs['v_rows'].reshape(db, ts, h_moba, HEAD_DIM))
        outs['wp'].append(fp['wkv'])
        outs['ws'].append(fs['wkv'])
        outs['sp'].append(fp['shift'])
        outs['ss'].append(fs['shift'])

    y_prompt = rms_norm_rows(yp.reshape(b * t, d), final_norm).reshape(b, t, d)
    y_sample = rms_norm_rows(ys.reshape(db * ts, d), final_norm).reshape(db, ts, d)
    st = lambda n: jnp.stack(outs[n])
    return (y_prompt, y_sample, st('kp'), st('vp'), st('ks'), st('vs'), st('wp'), st('ws'), st('sp'), st('ss'))
```

```python
import functools
import math

import jax
import jax.numpy as jnp
from jax import lax
from jax.experimental import pallas as pl
from jax.experimental.pallas import tpu as pltpu

F32 = jnp.float32
BF16 = jnp.bfloat16
HI = lax.Precision.HIGHEST

HEAD_DIM = 64
MOBA_BLOCK = 256
MOBA_TOPK = 3
ROPE_THETA = 10000.0
NORM_EPS = 1e-6
GN_EPS = 64e-5
NEG_INF = -1e30
DECAY_LORA = 64
ICLR_LORA = 64
GATE_LORA = 128
LANES = 128
WKV_CHUNK = 64

VMEM_LIMIT = 56 * 1024 * 1024


def _cparams(*sem):
    return pltpu.CompilerParams(dimension_semantics=sem, vmem_limit_bytes=VMEM_LIMIT)


def _row_tile(n, target):
    t = min(n, target)
    while n % t:
        t //= 2
    return t


def _nt_dot(a, b, precision=None):
    return lax.dot_general(a, b, (((1,), (1,)), ((), ())), precision=precision, preferred_element_type=F32)


def _ffn_kernel(x_ref, g_ref, wg_ref, wu_ref, wd_ref, o_ref, h_scr, acc_scr):
    j = pl.program_id(1)

    @pl.when(j == 0)
    def _():
        x = x_ref[...]
        ms = jnp.mean(x * x, axis=-1, keepdims=True)
        h_scr[...] = (x * lax.rsqrt(ms + NORM_EPS) * g_ref[...]).astype(BF16)
        acc_scr[...] = jnp.zeros_like(acc_scr)

    h = h_scr[...]
    a = jnp.dot(h, wg_ref[...], preferred_element_type=F32)
    b = jnp.dot(h, wu_ref[...], preferred_element_type=F32)
    u = a * jax.nn.sigmoid(a) * b
    acc_scr[...] += jnp.dot(u.astype(BF16), wd_ref[...], preferred_element_type=F32)

    @pl.when(j == pl.num_programs(1) - 1)
    def _():
        o_ref[...] = x_ref[...] + 0.5 * acc_scr[...]


def ffn_half(x, g, wg, wu, wd, *, tm_target=1024, tf=256):
    n, d = x.shape
    f = wg.shape[1]
    tm = _row_tile(n, tm_target)
    assert f % tf == 0
    return pl.pallas_call(
        _ffn_kernel,
        grid=(n // tm, f // tf),
        in_specs=[
            pl.BlockSpec((tm, d), lambda i, j: (i, 0)),
            pl.BlockSpec((1, d), lambda i, j: (0, 0)),
            pl.BlockSpec((d, tf), lambda i, j: (0, j)),
            pl.BlockSpec((d, tf), lambda i, j: (0, j)),
            pl.BlockSpec((tf, d), lambda i, j: (j, 0)),
        ],
        out_specs=pl.BlockSpec((tm, d), lambda i, j: (i, 0)),
        out_shape=jax.ShapeDtypeStruct((n, d), F32),
        scratch_shapes=[pltpu.VMEM((tm, d), BF16), pltpu.VMEM((tm, d), F32)],
        compiler_params=_cparams("parallel", "arbitrary"),
        name="ffn_half",
    )(x, g.reshape(1, d), wg, wu, wd)


def _proj_kernel(x_ref, g_ref, w_ref, o_ref, h_scr, *, normalize):
    @pl.when(pl.program_id(1) == 0)
    def _():
        x = x_ref[...]
        if normalize:
            ms = jnp.mean(x * x, axis=-1, keepdims=True)
            x = x * lax.rsqrt(ms + NORM_EPS) * g_ref[...]
        h_scr[...] = x.astype(BF16)

    o_ref[...] = jnp.dot(h_scr[...], w_ref[...], preferred_element_type=F32)


def norm_proj(x, g, w, *, normalize=True, tm_target=1024, tn=256):
    n, d = x.shape
    c = w.shape[1]
    tm = _row_tile(n, tm_target)
    assert c % tn == 0
    return pl.pallas_call(
        functools.partial(_proj_kernel, normalize=normalize),
        grid=(n // tm, c // tn),
        in_specs=[
            pl.BlockSpec((tm, d), lambda i, j: (i, 0)),
            pl.BlockSpec((1, d), lambda i, j: (0, 0)),
            pl.BlockSpec((d, tn), lambda i, j: (0, j)),
        ],
        out_specs=pl.BlockSpec((tm, tn), lambda i, j: (i, j)),
        out_shape=jax.ShapeDtypeStruct((n, c), F32),
        scratch_shapes=[pltpu.VMEM((tm, d), BF16)],
        compiler_params=_cparams("parallel", "arbitrary"),
        name="norm_proj",
    )(x, g.reshape(1, d), w)


def _rms_kernel(x_ref, g_ref, o_ref):
    x = x_ref[...]
    ms = jnp.mean(x * x, axis=-1, keepdims=True)
    o_ref[...] = x * lax.rsqrt(ms + NORM_EPS) * g_ref[...]


def rms_norm_rows(x, g, *, tm_target=1024):
    n, d = x.shape
    tm = _row_tile(n, tm_target)
    return pl.pallas_call(
        _rms_kernel,
        grid=(n // tm,),
        in_specs=[pl.BlockSpec((tm, d), lambda i: (i, 0)), pl.BlockSpec((1, d), lambda i: (0, 0))],
        out_specs=pl.BlockSpec((tm, d), lambda i: (i, 0)),
        out_shape=jax.ShapeDtypeStruct((n, d), F32),
        compiler_params=_cparams("parallel"),
        name="rms_norm",
    )(x, g.reshape(1, d))


def _head_block_diag(n):
    r = lax.broadcasted_iota(jnp.int32, (n, n), 0) // HEAD_DIM
    c = lax.broadcasted_iota(jnp.int32, (n, n), 1) // HEAD_DIM
    return (r == c).astype(F32)


def _prep_kernel(*refs, has_prev_tile, with_cum, n_heads):
    if has_prev_tile:
        p_ref, prev8_ref, pprev_ref = refs[:3]
        rest = refs[3:]
    else:
        p_ref, pprev_ref = refs[:2]
        prev8_ref = None
        rest = refs[2:]
    mu_ref, w0_ref, w2_ref, a0_ref, a2_ref, g2_ref, kk_ref, ka_ref = rest[:8]
    r_out, w_out, k_out, v_out, kk_out, kka_out, g_out = rest[8:15]
    c_out = rest[15] if with_cum else None
    d_r = n_heads * HEAD_DIM
    p = p_ref[0]
    tt = p.shape[0]
    prev_row = pprev_ref[0]
    if has_prev_tile:
        prev_row = jnp.where(pl.program_id(1) == 0, prev_row, prev8_ref[0, 7:8, :])
    row = lax.broadcasted_iota(jnp.int32, p.shape, 0)
    if tt % 8 == 0:
        rolled = pltpu.roll(p, 1, 0)
    else:
        rolled = jnp.concatenate([p[tt - 1:], p[:tt - 1]], axis=0)
    p_shift = jnp.where(row == 0, prev_row, rolled)
    xm = p + (p_shift - p) * mu_ref[...]
    r = xm[:, :d_r]
    k = xm[:, d_r:2 * d_r]
    v = xm[:, 2 * d_r:3 * d_r]
    c0 = 3 * d_r
    xw = xm[:, c0:c0 + DECAY_LORA]
    xa = xm[:, c0 + DECAY_LORA:c0 + DECAY_LORA + ICLR_LORA]
    xg = xm[:, c0 + DECAY_LORA + ICLR_LORA:]
    z = -(w0_ref[...] + jnp.dot(jnp.tanh(xw), w2_ref[...], precision=HI, preferred_element_type=F32))
    softplus = jnp.maximum(z, 0.0) + jnp.log(1.0 + jnp.exp(-jnp.abs(z)))
    log_decay = -jnp.exp(-softplus - 0.5)
    if with_cum:
        ci = lax.broadcasted_iota(jnp.int32, (tt, tt), 0)
        cj = lax.broadcasted_iota(jnp.int32, (tt, tt), 1)
        tri = ((ci // WKV_CHUNK == cj // WKV_CHUNK) & (cj <= ci)).astype(F32)
        cum = jnp.dot(tri, log_decay, precision=HI, preferred_element_type=F32)
    a =jax.nn.sigmoid(a0_ref[...] + jnp.dot(xa, a2_ref[...], precision=HI, preferred_element_type=F32))
    g = jnp.dot(jax.nn.sigmoid(xg), g2_ref[...], precision=HI, preferred_element_type=F32)
    kk = k * kk_ref[...]
    ss = jnp.dot(kk * kk, _head_block_diag(d_r), precision=HI, preferred_element_type=F32)
    kk = kk / jnp.maximum(jnp.sqrt(ss), 1e-12)
    k2 = k * (1.0 + (a - 1.0) * ka_ref[...])
    kka = kk * a
    for h in range(n_heads):
        sl = slice(h * HEAD_DIM, (h + 1) * HEAD_DIM)
        r_out[0, h] = r[:, sl]
        w_out[0, h] = log_decay[:, sl]
        if with_cum:
            c_out[0, h] = cum[:, sl]
        k_out[0, h] = k2[:, sl]
        v_out[0, h] = v[:, sl]
        kk_out[0, h] = kk[:, sl]
        kka_out[0, h] = kka[:, sl]
        g_out[0, h] = g[:, sl]


def rwkv_prep(pr, p_prev, lp, *, tt_target=256):
    b, t, c = pr.shape
    d_r = lp['w0'].shape[0]
    n_heads = d_r // HEAD_DIM
    tt = _row_tile(t, tt_target)
    nt = t // tt
    has_prev = nt > 1
    with_cum = tt % WKV_CHUNK == 0
    n_out = 8 if with_cum else 7
    row = lambda x: x.reshape(1, -1)
    full = lambda shape: pl.BlockSpec(shape, lambda bi, i: (0,) * len(shape))
    in_specs = [pl.BlockSpec((1, tt, c), lambda bi, i: (bi, i, 0))]
    args = [pr]
    if has_prev:
        in_specs.append(pl.BlockSpec((1, 8, c), lambda bi, i: (bi, jnp.maximum(i * (tt // 8) - 1, 0), 0)))
        args.append(pr)
    in_specs.append(pl.BlockSpec((1, 1, c), lambda bi, i: (bi, 0, 0)))
    args.append(p_prev.reshape(b, 1, c))
    params = [row(lp['mu']), row(lp['w0']), lp['w2'], row(lp['a0']), lp['a2'], lp['g2'], row(lp['k_k']), row(lp['k_a'])]
    in_specs += [full(x.shape) for x in params]
    args += params
    hm = jax.ShapeDtypeStruct((b, n_heads, t, HEAD_DIM), F32)
    hm_spec = pl.BlockSpec((1, n_heads, tt, HEAD_DIM), lambda bi, i: (bi, 0, i, 0))
    return pl.pallas_call(
        functools.partial(_prep_kernel, has_prev_tile=has_prev, with_cum=with_cum, n_heads=n_heads),
        grid=(b, nt),
        in_specs=in_specs,
        out_specs=[hm_spec] * n_out,
        out_shape=[hm] * n_out,
        compiler_params=_cparams("parallel", "arbitrary"),
        name="rwkv_prep",
    )(*args)


def _scan_kernel(r_ref, lw_ref, k_ref, v_ref, kk_ref, kka_ref, s0_ref, y_ref, s_out_ref, s_scr, *, n_heads, tc):
    @pl.when(pl.program_id(1) == 0)
    def _():
        s_scr[...] = s0_ref[0]

    eye = (lax.broadcasted_iota(jnp.int32, (HEAD_DIM, HEAD_DIM), 0)
           == lax.broadcasted_iota(jnp.int32, (HEAD_DIM, HEAD_DIM), 1))

    def body(t, carry):
        for h in range(n_heads):
            s = s_scr[h]
            row = lambda ref: ref[0, h, pl.ds(t, 1), :]
            sa = -jnp.sum(s * row(kk_ref), axis=1, keepdims=True)
            v_col = jnp.sum(jnp.where(eye, row(v_ref), 0.0), axis=1, keepdims=True)
            s = s * jnp.exp(row(lw_ref)) + sa * row(kka_ref) + v_col * row(k_ref)
            s_scr[h] = s
            y_col = jnp.sum(s * row(r_ref), axis=1, keepdims=True)
            y_ref[0, h, pl.ds(t, 1), :] = jnp.sum(jnp.where(eye, y_col, 0.0), axis=0, keepdims=True)
        return carry

    lax.fori_loop(0, tc, body, 0)
    s_out_ref[0] = s_scr[...]


def _bdot(a, b, contract, precision=None):
    if precision is None:
        a, b = a.astype(BF16), b.astype(BF16)
    return lax.dot_general(a, b, (((contract[0],), (contract[1],)), ((0,), (0,))), precision=precision,
                           preferred_element_type=F32)


_NN, _NT, _TN = (2, 1), (2, 2), (1, 1)


def _split(x):
    hi = x.astype(BF16)
    return hi, (x - hi.astype(F32)).astype(BF16)


def _bdot_split(a, b, contract):
    (ah, al), (bh, bl) = a, b
    return _bdot(ah, bh, contract) + (_bdot(ah, bl, contract) + _bdot(al, bh, contract))


def _unit_lower_inverse(t_strict):
    n = t_strict.shape[1]
    ri = lax.broadcasted_iota(jnp.int32, (n, n), 0)
    ci = lax.broadcasted_iota(jnp.int32, (n, n), 1)
    same = lambda size: (ri // size) == (ci // size)
    mm = lambda x, y: _bdot_split(x, y, _NN)
    neg32 = jnp.where(same(8), -t_strict, 0.0)
    neg = _split(neg32)
    n2 = _split(mm(neg, neg))
    n4 = _split(mm(n2, n2))
    inv = (ri == ci).astype(F32) + neg32
    inv = inv + mm(_split(inv), n2)
    inv = inv + mm(_split(inv), n4)
    for size in (16, 32, 64):
        off = _split(jnp.where(same(size) & jnp.logical_not(same(size // 2)), t_strict, 0.0))
        inv_s = _split(inv)
        inv = inv - mm(inv_s, _split(mm(off, inv_s)))
    return inv


def _chunk_scan_kernel(r_ref, lw_ref, c_ref, k_ref, v_ref, kk_ref, kka_ref, s0_ref, y_ref, s_out_ref, s_scr,
                       *, n_heads, n_chunks):
    L = WKV_CHUNK

    @pl.when(pl.program_id(1) == 0)
    def _():
        s_scr[...] = s0_ref[0]

    ri = lax.broadcasted_iota(jnp.int32, (2 * L, 2 * L), 0)
    ci = lax.broadcasted_iota(jnp.int32, (2 * L, 2 * L), 1)
    keep = ((ri < L) & ((ci % L) < ri)) | ((ri >= L) & ((ci % L) <= (ri - L)))

    def body(cidx, carry):
        rows = pl.ds(pl.multiple_of(cidx * L, L), L)
        ld = lambda ref: ref[0, :, rows, :]
        r, lw, c, k, v, a, b = (ld(x) for x in (r_ref, lw_ref, c_ref, k_ref, v_ref, kk_ref, kka_ref))
        c_last = c[:, L - 1:L, :]
        inv_g = jnp.exp(-c)
        a_t = a * jnp.exp(c - lw)
        b_t = b * inv_g
        k_t = k * inv_g
        r_t = r * jnp.exp(c)
        to_end = jnp.exp(c_last - c)
        b_end = b * to_end
        k_end = k * to_end
        coef = _bdot_split(_split(jnp.concatenate([a_t, r_t], axis=1)),
                           _split(jnp.concatenate([b_t, k_t], axis=1)), _NT)
        coef = jnp.where(keep, coef, 0.0)
        t_ab, t_ak = coef[:, :L, :L], coef[:, :L, L:]
        a_rb, a_rk = coef[:, L:, :L], coef[:, L:, L:]
        m_inv = _unit_lower_inverse(t_ab)
        w_mat = _bdot(m_inv, a_t, _NN)
        u_mat = -_bdot(m_inv, _bdot(t_ak, v, _NN), _NN)
        y_intra = _bdot(a_rb, u_mat, _NN) + _bdot(a_rk, v, _NN)
        r_w = r_t - _bdot(a_rb, w_mat, _NN)
        h_mat = _bdot(u_mat, b_end, _TN) + _bdot(v, k_end, _TN)
        s0 = s_scr[...]
        y_ref[0, :, rows, :] = _bdot(r_w, s0, _NT) + y_intra
        p = _bdot(s0, w_mat, _NT)
        s_scr[...] = s0 * jnp.exp(c_last) - _bdot(p, b_end, _NN) + h_mat
        return carry

    lax.fori_loop(0, n_chunks, body, 0)
    s_out_ref[0] = s_scr[...]


def wkv_chunk_scan(r, lw, c, k, v, kk, kka, s0, *, tc_target=256):
    b, h, t, n = r.shape
    tc = _row_tile(t, tc_target)
    assert tc % WKV_CHUNK == 0 and n == WKV_CHUNK
    seq = pl.BlockSpec((1, h, tc, n), lambda bi, i: (bi, 0, i, 0))
    st = pl.BlockSpec((1, h, n, n), lambda bi, i: (bi, 0, 0, 0))
    return pl.pallas_call(
        functools.partial(_chunk_scan_kernel, n_heads=h, n_chunks=tc // WKV_CHUNK),
        grid=(b, t // tc),
        in_specs=[seq] * 7 + [st],
        out_specs=[seq, st],
        out_shape=[jax.ShapeDtypeStruct((b, h, t, n), F32), jax.ShapeDtypeStruct((b, h, n, n), F32)],
        scratch_shapes=[pltpu.VMEM((h, n, n), F32)],
        compiler_params=_cparams("parallel", "arbitrary"),
        name="wkv_chunk_scan",
    )(r, lw, c, k, v, kk, kka, s0)


def wkv_scan(r, w, k, v, kk, kka, s0, *, tc_target=256):
    b, h, t, n = r.shape
    tc = _row_tile(t, tc_target)
    seq = pl.BlockSpec((1, h, tc, n), lambda bi, c: (bi, 0, c, 0))
    st = pl.BlockSpec((1, h, n, n), lambda bi, c: (bi, 0, 0, 0))
    return pl.pallas_call(
        functools.partial(_scan_kernel, n_heads=h, tc=tc),
        grid=(b, t // tc),
        in_specs=[seq] * 6 + [st],
        out_specs=[seq, st],
        out_shape=[jax.ShapeDtypeStruct((b, h, t, n), F32), jax.ShapeDtypeStruct((b, h, n, n), F32)],
        scratch_shapes=[pltpu.VMEM((h, n, n), F32)],
        compiler_params=_cparams("parallel", "arbitrary"),
        name="wkv_scan",
    )(r, w, k, v, kk, kka, s0)


def _rope_kernel(p_ref, cos_ref, sin_ref, krow_ref, q_out, k_out=None, vt_out=None, *, n_heads, head_major):
    d_m = n_heads * HEAD_DIM
    half = HEAD_DIM // 2
    cos = cos_ref[...]
    sin = sin_ref[...]
    first_half = (lax.broadcasted_iota(jnp.int32, cos.shape, 1) % HEAD_DIM) < half

    def rot(x):
        if x.shape[0] % 8 == 0:
            fwd = pltpu.roll(x, d_m - half, 1)
            bwd = pltpu.roll(x, half, 1)
        else:
            fwd = jnp.concatenate([x[:, half:], x[:, :half]], axis=1)
            bwd = jnp.concatenate([x[:, d_m - half:], x[:, :d_m - half]], axis=1)
        return x * cos + jnp.where(first_half, fwd, bwd) * sin

    p = p_ref[0]
    q = rot(p[:, :d_m])
    k = rot(p[:, d_m:2 * d_m])
    krow_ref[0] = k
    if not head_major:
        q_out[0] = q
        return
    v = p[:, 2 * d_m:]
    for h in range(n_heads):
        sl = slice(h * HEAD_DIM, (h + 1) * HEAD_DIM)
        q_out[0, h] = q[:, sl]
        k_out[0, h] = k[:, sl].astype(BF16)
        for jb in range(p.shape[0] // MOBA_BLOCK):
            vt_out[0, h, jb] = v[jb * MOBA_BLOCK:(jb + 1) * MOBA_BLOCK, sl].T.astype(BF16)


def rope_qkv(pm, cos, sin, *, head_major, tt_target=512):
    b, t, c = pm.shape
    d_m = c // 3
    n_heads = d_m // HEAD_DIM
    tt = _row_tile(t, tt_target)
    rows_spec = pl.BlockSpec((1, tt, d_m), lambda bi, i: (bi, i, 0))
    rows = jax.ShapeDtypeStruct((b, t, d_m), F32)
    if head_major:
        assert tt % MOBA_BLOCK == 0
        hm_spec = pl.BlockSpec((1, n_heads, tt, HEAD_DIM), lambda bi, i: (bi, 0, i, 0))
        hm = lambda dt: jax.ShapeDtypeStruct((b, n_heads, t, HEAD_DIM), dt)
        nbt = tt // MOBA_BLOCK
        out_specs = [rows_spec, hm_spec, hm_spec,
                     pl.BlockSpec((1, n_heads, nbt, HEAD_DIM, MOBA_BLOCK), lambda bi, i: (bi, 0, i, 0, 0))]
        out_shape = [rows, hm(F32), hm(BF16),
                     jax.ShapeDtypeStruct((b, n_heads, t // MOBA_BLOCK, HEAD_DIM, MOBA_BLOCK), BF16)]
    else:
        out_specs = [rows_spec, rows_spec]
        out_shape = [rows, rows]
    return pl.pallas_call(
        functools.partial(_rope_kernel, n_heads=n_heads, head_major=head_major),
        grid=(b, t // tt),
        in_specs=[
            pl.BlockSpec((1, tt, c), lambda bi, i: (bi, i, 0)),
            pl.BlockSpec((tt, d_m), lambda bi, i: (i, 0)),
            pl.BlockSpec((tt, d_m), lambda bi, i: (i, 0)),
        ],
        out_specs=out_specs,
        out_shape=out_shape,
        compiler_params=_cparams("parallel", "parallel"),
        name="rope_qkv",
    )(pm, cos, sin)


def rope_tables(pos, n_heads):
    half = HEAD_DIM // 2
    inv_freq = ROPE_THETA ** (-jnp.arange(half, dtype=F32) / half)
    ang = pos.astype(F32)[:, None] * inv_freq[None, :]
    cos = jnp.cos(ang)
    sin = jnp.sin(ang)
    cos = jnp.tile(jnp.concatenate([cos, cos], axis=1), (1, n_heads))
    sin = jnp.tile(jnp.concatenate([-sin, sin], axis=1), (1, n_heads))
    return cos, sin


def _kmean_kernel(k_ref, o_ref, *, nblk):
    for j in range(nblk):
        o_ref[0, j:j + 1, :] = jnp.mean(k_ref[0, j * MOBA_BLOCK:(j + 1) * MOBA_BLOCK, :], axis=0, keepdims=True)


def block_means(k_rows):
    b, t, d_m = k_rows.shape
    nb = t // MOBA_BLOCK
    nblk = 8 if nb % 8 == 0 else nb
    return pl.pallas_call(
        functools.partial(_kmean_kernel, nblk=nblk),
        grid=(b, nb // nblk),
        in_specs=[pl.BlockSpec((1, nblk * MOBA_BLOCK, d_m), lambda bi, i: (bi, i, 0))],
        out_specs=pl.BlockSpec((1, nblk, d_m), lambda bi, i: (bi, i, 0)),
        out_shape=jax.ShapeDtypeStruct((b, nb, d_m), F32),
        compiler_params=_cparams("parallel", "parallel"),
        name="block_means",
    )(k_rows)


def _top_blocks(gate, valid, n_sel):
    ax = gate.ndim - 1
    nb = gate.shape[ax]
    blk = lax.broadcasted_iota(jnp.int32, gate.shape, ax)
    avail = valid
    sel = jnp.zeros(gate.shape, F32)
    for _ in range(n_sel):
        g = jnp.where(avail, gate, -jnp.inf)
        m = jnp.max(g, axis=ax, keepdims=True)
        first = jnp.min(jnp.where((g == m) & avail, blk, nb), axis=ax, keepdims=True)
        pick = blk == first
        sel = jnp.where(pick, 1.0, sel)
        avail = avail & jnp.logical_not(pick)
    return sel


def _top_blocks_t(gate, valid, n_sel):
    ax = gate.ndim - 2
    nb = gate.shape[ax]
    blk = lax.broadcasted_iota(jnp.int32, gate.shape, ax)
    avail = jnp.broadcast_to(valid, gate.shape)
    sel = jnp.zeros(gate.shape, F32)
    for _ in range(n_sel):
        g = jnp.where(avail, gate, -jnp.inf)
        m = jnp.max(g, axis=ax, keepdims=True)
        first = jnp.min(jnp.where((g == m) & avail, blk, nb), axis=ax, keepdims=True)
        pick = blk == first
        sel = jnp.where(pick, 1.0, sel)
        avail = avail & jnp.logical_not(pick)
    return sel


MOBA_HEADS_PER_STEP = 4
SUBLANES = 8
MOBA_BLOCKS_PER_ITER = SUBLANES // 2


def _moba_prompt_kernel(q_ref, k_ref, vt_ref, km_ref, o_ref, sel_scr, *, n_sel):
    i = pl.program_id(2)
    hb = q_ref.shape[1]
    nb = km_ref.shape[2]
    q32 = q_ref[0]
    q = (q32 * (HEAD_DIM ** -0.5 * math.log2(math.e))).astype(BF16)

    def scores_t(j0, n_blocks):
        rows = pl.ds(pl.multiple_of(j0 * MOBA_BLOCK, MOBA_BLOCK), n_blocks * MOBA_BLOCK)
        return _bdot(k_ref[0, :, rows, :], q, _NT)

    def weighted_values_t(j0, p):
        out = None
        for jb in range(p.shape[1] // MOBA_BLOCK):
            pv = _bdot(vt_ref[0, :, j0 + jb], p[:, jb * MOBA_BLOCK:(jb + 1) * MOBA_BLOCK], _NN)
            out = pv if out is None else out + pv
        return out

    key_pos = lax.broadcasted_iota(jnp.int32, (MOBA_BLOCK, MOBA_BLOCK), 0)
    q_pos = lax.broadcasted_iota(jnp.int32, (MOBA_BLOCK, MOBA_BLOCK), 1)
    s = jnp.where(key_pos <= q_pos, scores_t(i, 1), NEG_INF)
    m = jnp.max(s, axis=1, keepdims=True)
    p = jnp.exp2(s - m)
    l = jnp.sum(p, axis=1, keepdims=True)
    acc = weighted_values_t(i, p)

    if n_sel > 0:
        gate_t = _bdot(km_ref[0], q32, _NT, precision=HI)
        blk = lax.broadcasted_iota(jnp.int32, (nb, MOBA_BLOCK), 0)
        sel_scr[...] = _top_blocks_t(gate_t, blk < i, n_sel)
        kb = MOBA_BLOCKS_PER_ITER

        def body(jj, carry):
            m, l, acc = carry
            j0 = jj * kb
            sel8 = sel_scr[:, pl.ds(pl.multiple_of((jj // 2) * 8, 8), 8), :]
            chosen = jnp.where(jj % 2 == 0, sel8[:, :kb], sel8[:, kb:]) > 0.0
            s = scores_t(j0, kb).reshape(hb, kb, MOBA_BLOCK, MOBA_BLOCK)
            s = jnp.where(chosen[:, :, None, :], s, NEG_INF).reshape(hb, kb * MOBA_BLOCK, MOBA_BLOCK)
            m_new = jnp.maximum(m, jnp.max(s, axis=1, keepdims=True))
            alpha = jnp.exp2(m - m_new)
            p = jnp.exp2(s - m_new)
            l = alpha * l + jnp.sum(p, axis=1, keepdims=True)
            acc = alpha * acc + weighted_values_t(j0, p)
            return m_new, l, acc

        m, l, acc = lax.fori_loop(0, (i + kb - 1) // kb, body, (m, l, acc))

    o_t = acc / l
    for h in range(hb):
        o_ref[0, h] = o_t[h].T


def moba_prompt(q, k, vt, kmean):
    b, h, t, d = q.shape
    assert t % MOBA_BLOCK == 0
    nb = t // MOBA_BLOCK
    n_sel = min(MOBA_TOPK, nb - 1)
    nb_pad = -(-nb // 8) * 8
    kmean = jnp.pad(kmean, ((0, 0), (0, 0), (0, nb_pad - nb), (0, 0)))
    hb = MOBA_HEADS_PER_STEP
    assert h % hb == 0 and nb % MOBA_BLOCKS_PER_ITER == 0
    q_spec = pl.BlockSpec((1, hb, MOBA_BLOCK, d), lambda bi, hi, i: (bi, hi, i, 0))
    return pl.pallas_call(
        functools.partial(_moba_prompt_kernel, n_sel=n_sel),
        grid=(b, h // hb, nb),
        in_specs=[
            q_spec,
            pl.BlockSpec((1, hb, t, d), lambda bi, hi, i: (bi, hi, 0, 0)),
            pl.BlockSpec((1, hb, nb, d, MOBA_BLOCK), lambda bi, hi, i: (bi, hi, 0, 0, 0)),
            pl.BlockSpec((1, hb, nb_pad, d), lambda bi, hi, i: (bi, hi, 0, 0)),
        ],
        out_specs=q_spec,
        out_shape=jax.ShapeDtypeStruct((b, h, t, d), F32),
        scratch_shapes=[pltpu.VMEM((hb, nb_pad, MOBA_BLOCK), F32)],
        compiler_params=_cparams("parallel", "parallel", "arbitrary"),
        name="moba_prompt",
    )(q, k, vt, kmean)


PAGES_PER_STEP = 16
SAMPLE_BLOCKS_PER_STEP = 4


def _sample_kmean_kernel(pt_ref, *refs, ppb, n_heads):
    pages, o_ref = refs[:-1], refs[-1]
    nblk = len(pages) // ppb
    page_size = pages[0].shape[2] // n_heads
    page_sum = lambda ref: jnp.sum(ref[0, 0].reshape(page_size, n_heads, HEAD_DIM), axis=0)
    for j in range(nblk):
        s = page_sum(pages[j * ppb])
        for q in range(1, ppb):
            s = s + page_sum(pages[j * ppb + q])
        o_ref[0, j * n_heads:(j + 1) * n_heads, :] = s / (ppb * page_size)


def sample_block_means(cache_k, layer, page_table, n_heads):
    _, _, page_rows, d = cache_k.shape
    page_size = page_rows // n_heads
    db, n_pages = page_table.shape
    ppb = MOBA_BLOCK // page_size
    n_full = (n_pages * page_size) // MOBA_BLOCK
    pps = PAGES_PER_STEP
    assert n_pages % pps == 0 and pps % ppb == 0

    def page_spec(q):
        return pl.BlockSpec((1, 1, page_rows, d), lambda bi, g, pt: (layer, pt[bi, g * pps + q], 0, 0))

    return pl.pallas_call(
        functools.partial(_sample_kmean_kernel, ppb=ppb, n_heads=n_heads),
        grid_spec=pltpu.PrefetchScalarGridSpec(
            num_scalar_prefetch=1,
            grid=(db, n_pages // pps),
            in_specs=[page_spec(q) for q in range(pps)],
            out_specs=pl.BlockSpec((1, pps // ppb * n_heads, d), lambda bi, g, pt: (bi, g, 0)),
        ),
        out_shape=jax.ShapeDtypeStruct((db, n_full * n_heads, d), F32),
        compiler_params=_cparams("parallel", "arbitrary"),
        name="sample_block_means",
    )(page_table, *([cache_k] * pps))


def _moba_sample_kernel(pt_ref, q_ref, kn_ref, vn_ref, km_ref, *refs, ppb, n_heads, n_sel):
    n_pages = (len(refs) - 5) // 2
    k_pages = refs[:n_pages]
    v_pages = refs[n_pages:2 * n_pages]
    o_ref = refs[2 * n_pages]
    sel_scr, m_scr, l_scr, acc_scr = refs[2 * n_pages + 1:]
    n = pl.program_id(1)
    q32 = q_ref[0]
    q = (q32 * HEAD_DIM ** -0.5).astype(BF16)
    page_size = k_pages[0].shape[2] // n_heads
    heads = lambda ref: jnp.stack([ref[0, 0, pl.ds(h, page_size, stride=n_heads), :] for h in range(n_heads)],
                                  axis=0)

    @pl.when(n == 0)
    def _():
        if n_sel > 0:
            n_full = km_ref.shape[1] // n_heads
            km = jnp.stack([km_ref[0, pl.ds(h, n_full, stride=n_heads), :] for h in range(n_heads)],
                           axis=0)
            gate = _bdot(q32, km, _NT, precision=HI)
            sel_scr[...] = _top_blocks(gate, jnp.ones(gate.shape, jnp.bool_), n_sel)
        s = _bdot(q, kn_ref[0], _NT)
        q_t = lax.broadcasted_iota(jnp.int32, s.shape, 1)
        k_t = lax.broadcasted_iota(jnp.int32, s.shape, 2)
        s = jnp.where(k_t <= q_t, s, NEG_INF)
        m = jnp.max(s, axis=2, keepdims=True)
        p = jnp.exp(s - m)
        m_scr[...] = m
        l_scr[...] = jnp.sum(p, axis=2, keepdims=True)
        acc_scr[...] = _bdot(p, vn_ref[0], _NN)

    if n_sel > 0:
        sel = sel_scr[...]
        blk = lax.broadcasted_iota(jnp.int32, sel.shape, 2)
        scores = []
        for pg in range(n_pages):
            b = n * (n_pages // ppb) + pg // ppb
            chosen = jnp.sum(jnp.where(blk == b, sel, 0.0), axis=2, keepdims=True) > 0.0
            scores.append(jnp.where(chosen, _bdot(q, heads(k_pages[pg]), _NT), NEG_INF))
        s = jnp.concatenate(scores, axis=2)
        m = m_scr[...]
        m_new = jnp.maximum(m, jnp.max(s, axis=2, keepdims=True))
        alpha = jnp.exp(m - m_new)
        p = jnp.exp(s - m_new)
        m_scr[...] = m_new
        l_scr[...] = alpha * l_scr[...] + jnp.sum(p, axis=2, keepdims=True)
        acc = alpha * acc_scr[...]
        for pg in range(n_pages):
            acc = acc + _bdot(p[:, :, pg * page_size:(pg + 1) * page_size], heads(v_pages[pg]), _NN)
        acc_scr[...] = acc

    @pl.when(n == pl.num_programs(1) - 1)
    def _():
        o_ref[0] = acc_scr[...] / l_scr[...]


def moba_sample(cache_k, cache_v, layer, page_table, q_rows, k_rows, v_rows, kmean):
    _, _, page_rows, d = cache_k.shape
    db, t, d_m = q_rows.shape
    n_heads = d_m // d
    page_size = page_rows // n_heads
    n_pages = page_table.shape[1]
    ppb = MOBA_BLOCK // page_size
    n_full = (n_pages * page_size) // MOBA_BLOCK
    assert n_full * ppb == n_pages, "past rows inside the current block are not supported"
    n_sel = min(MOBA_TOPK, n_full)
    tp = -(-t // SUBLANES) * SUBLANES
    hm = lambda x: jnp.pad(_to_heads(x), ((0, 0), (0, 0), (0, tp - t), (0, 0)))
    new_spec = pl.BlockSpec((1, n_heads, tp, d), lambda bi, n, pt: (bi, 0, 0, 0))

    pps = SAMPLE_BLOCKS_PER_STEP * ppb
    assert n_pages % pps == 0

    def page_spec(q):
        return pl.BlockSpec((1, 1, page_rows, d), lambda bi, n, pt: (layer, pt[bi, n * pps + q], 0, 0))

    o = pl.pallas_call(
        functools.partial(_moba_sample_kernel, ppb=ppb, n_heads=n_heads, n_sel=n_sel),
        grid_spec=pltpu.PrefetchScalarGridSpec(
            num_scalar_prefetch=1,
            grid=(db, n_pages // pps),
            in_specs=[new_spec, new_spec, new_spec,
                      pl.BlockSpec((1, n_full * n_heads, d), lambda bi, n, pt: (bi, 0, 0))]
            + [page_spec(q) for q in range(pps)] * 2,
            out_specs=new_spec,
            scratch_shapes=[
                pltpu.VMEM((n_heads, tp, n_full), F32),
                pltpu.VMEM((n_heads, tp, 1), F32),
                pltpu.VMEM((n_heads, tp, 1), F32),
                pltpu.VMEM((n_heads, tp, d), F32),
            ],
        ),
        out_shape=jax.ShapeDtypeStruct((db, n_heads, tp, d), F32),
        compiler_params=_cparams("parallel", "arbitrary"),
        name="moba_sample",
    )(page_table, hm(q_rows), hm(k_rows), hm(v_rows), kmean, *([cache_k] * pps), *([cache_v] * pps))
    return o[:, :, :t]


def _mix_out_kernel(y_ref, r_ref, k_ref, v_ref, g_ref, om_ref, x_ref, wo_ref, lnw_ref, lnb_ref, rk_ref, o_ref,
                    *, h_rwkv, h_moba):
    acc = x_ref[0]
    for h in range(h_rwkv):
        y = y_ref[0, h]
        mean = jnp.mean(y, axis=-1, keepdims=True)
        yc = y - mean
        var = jnp.mean(yc * yc, axis=-1, keepdims=True)
        yn = yc * lax.rsqrt(var + GN_EPS) * lnw_ref[h:h + 1, :] + lnb_ref[h:h + 1, :]
        bonus = jnp.sum(r_ref[0, h] * k_ref[0, h] * rk_ref[h:h + 1, :], axis=-1, keepdims=True) * v_ref[0, h]
        o = ((yn + bonus) * g_ref[0, h]).astype(BF16)
        acc = acc + jnp.dot(o, wo_ref[h * HEAD_DIM:(h + 1) * HEAD_DIM, :], preferred_element_type=F32)
    for h in range(h_moba):
        row0 = (h_rwkv + h) * HEAD_DIM
        acc = acc + jnp.dot(om_ref[0, h].astype(BF16), wo_ref[row0:row0 + HEAD_DIM, :], preferred_element_type=F32)
    o_ref[0] = acc


def mix_out(y, r, k, v, g, o_moba, x, w_out, ln_w, ln_b, r_k, *, tt_target=512):
    b, h_rwkv, t, n = y.shape
    h_moba = o_moba.shape[1]
    d = x.shape[2]
    tt = _row_tile(t, tt_target)
    hm_r = pl.BlockSpec((1, h_rwkv, tt, n), lambda bi, i: (bi, 0, i, 0))
    hm_m = pl.BlockSpec((1, h_moba, tt, n), lambda bi, i: (bi, 0, i, 0))
    full = lambda shape: pl.BlockSpec(shape, lambda bi, i: (0,) * len(shape))
    params = [w_out, ln_w.reshape(h_rwkv, n), ln_b.reshape(h_rwkv, n), r_k.reshape(h_rwkv, n)]
    return pl.pallas_call(
        functools.partial(_mix_out_kernel, h_rwkv=h_rwkv, h_moba=h_moba),
        grid=(b, t // tt),
        in_specs=[hm_r] * 5 + [hm_m, pl.BlockSpec((1, tt, d), lambda bi, i: (bi, i, 0))] + [full(p.shape) for p in params],
        out_specs=pl.BlockSpec((1, tt, d), lambda bi, i: (bi, i, 0)),
        out_shape=jax.ShapeDtypeStruct((b, t, d), F32),
        compiler_params=_cparams("parallel", "parallel"),
        name="mix_out",
    )(y, r, k, v, g, o_moba, x, *params)


def _to_heads(x_rows):
    b, t, dm = x_rows.shape
    return x_rows.reshape(b, t, dm // HEAD_DIM, HEAD_DIM).transpose(0, 2, 1, 3)


def _mixer_front(y, lw, p_prev_rows, wkv_prev, cos, sin, *, head_major):
    b, t, d = y.shape
    flat = y.reshape(b * t, d)
    pr = norm_proj(flat, lw['mix_norm'], lw['w_in_r']).reshape(b, t, -1)
    pm = norm_proj(flat, lw['mix_norm'], lw['w_in_m']).reshape(b, t, -1)
    shift = rms_norm_rows(y[:, -1], lw['mix_norm'])
    r, log_w, k, v, kk, kka, g, *cum = rwkv_prep(pr, p_prev_rows, lw)
    if cum:
        y_wkv, wkv_new = wkv_chunk_scan(r, log_w, cum[0], k, v, kk, kka, wkv_prev)
    else:
        y_wkv, wkv_new = wkv_scan(r, log_w, k, v, kk, kka, wkv_prev)
    k_rows, *moba_in = rope_qkv(pm, cos, sin, head_major=head_major)
    d_m = k_rows.shape[2]
    v_rows = pm[:, :, 2 * d_m:]
    return dict(r=r, k=k, v=v, g=g, y=y_wkv, wkv=wkv_new, shift=shift, k_rows=k_rows, v_rows=v_rows, moba_in=moba_in)


@jax.jit
def kernel(x_prompt, x_sample, cache_k, cache_v, state_wkv, state_shift, page_table, ffn1_norm, ffn1_w_gate,
           ffn1_w_up, ffn1_w_down, mix_norm, w_in, w_out, rwkv_mu, rwkv_w0, rwkv_w2, rwkv_a0, rwkv_a2, rwkv_g2,
           rwkv_k_k, rwkv_k_a, rwkv_r_k, rwkv_ln_w, rwkv_ln_b, ffn2_norm, ffn2_w_gate, ffn2_w_up, ffn2_w_down,
           final_norm):
    b, t, d = x_prompt.shape
    db, ts, _ = x_sample.shape
    depth = w_in.shape[0]
    d_r = rwkv_w0.shape[1]
    h_rwkv = d_r // HEAD_DIM
    c_r = rwkv_mu.shape[1]
    d_m = (w_in.shape[2] - c_r) // 3
    h_moba = d_m // HEAD_DIM
    n_pool, page_size = cache_k.shape[1], cache_k.shape[2]
    past_len = page_table.shape[1] * page_size
    ck = cache_k.reshape(depth, n_pool, page_size * h_moba, HEAD_DIM)
    cv = cache_v.reshape(depth, n_pool, page_size * h_moba, HEAD_DIM)
    cos_p, sin_p = rope_tables(jnp.arange(t), h_moba)
    cos_s, sin_s = rope_tables(past_len + jnp.arange(ts), h_moba)

    yp, ys = x_prompt, x_sample
    outs = {n: [] for n in ('kp', 'vp', 'ks', 'vs', 'wp', 'ws', 'sp', 'ss')}
    for l in range(depth):
        lw = {'mix_norm': mix_norm[l], 'w_in_r': w_in[l, :, :c_r].astype(BF16), 'w_in_m': w_in[l, :, c_r:].astype(BF16),
              'mu': rwkv_mu[l], 'w0': rwkv_w0[l], 'w2': rwkv_w2[l], 'a0': rwkv_a0[l], 'a2': rwkv_a2[l],
              'g2': rwkv_g2[l], 'k_k': rwkv_k_k[l], 'k_a': rwkv_k_a[l]}
        wo = w_out[l].astype(BF16)
        f1 = (ffn1_norm[l], ffn1_w_gate[l].astype(BF16), ffn1_w_up[l].astype(BF16), ffn1_w_down[l].astype(BF16))
        f2 = (ffn2_norm[l], ffn2_w_gate[l].astype(BF16), ffn2_w_up[l].astype(BF16), ffn2_w_down[l].astype(BF16))

        yp = ffn_half(yp.reshape(b * t, d), *f1).reshape(b, t, d)
        ys = ffn_half(ys.reshape(db * ts, d), *f1).reshape(db, ts, d)

        fp = _mixer_front(yp, lw, jnp.zeros((b, c_r), F32), jnp.zeros((b, h_rwkv, HEAD_DIM, HEAD_DIM), F32),
                          cos_p, sin_p, head_major=True)
        p_prev_s = norm_proj(state_shift[l], lw['mix_norm'], lw['w_in_r'], normalize=False)
        fs = _mixer_front(ys, lw, p_prev_s, state_wkv[l], cos_s, sin_s, head_major=False)

        kmean_p = _to_heads(block_means(fp['k_rows']))
        om_p = moba_prompt(*fp['moba_in'], kmean_p)
        kmean_s = sample_block_means(ck, l, page_table, h_moba)
        om_s = moba_sample(ck, cv, l, page_table, fs['moba_in'][0], fs['k_rows'], fs['v_rows'], kmean_s)

        mix = lambda f, om, y: mix_out(f['y'], f['r'], f['k'], f['v'], f['g'], om, y, wo,
                                       rwkv_ln_w[l], rwkv_ln_b[l], rwkv_r_k[l])
        yp = mix(fp, om_p, yp)
        ys = mix(fs, om_s, ys)

        yp = ffn_half(yp.reshape(b * t, d), *f2).reshape(b, t, d)
        ys = ffn_half(ys.reshape(db * ts, d), *f2).reshape(db, ts, d)

        outs['kp'].append(fp['k_rows'].reshape(b, t, h_moba, HEAD_DIM))
        outs['vp'].append(fp['v_rows'].reshape(b, t, h_moba, HEAD_DIM))
        outs['ks'].append(fs['k_rows'].reshape(db, ts, h_moba, HEAD_DIM))
        outs['vs'].append(fs['v_rows'].reshape(db, ts, h_moba, HEAD_DIM))
        outs['wp'].append(fp['wkv'])
        outs['ws'].append(fs['wkv'])
        outs['sp'].append(fp['shift'])
        outs['ss'].append(fs['shift'])

    y_prompt = rms_norm_rows(yp.reshape(b * t, d), final_norm).reshape(b, t, d)
    y_sample = rms_norm_rows(ys.reshape(db * ts, d), final_norm).reshape(db, ts, d)
    st = lambda n: jnp.stack(outs[n])
    return (y_prompt, y_sample, st('kp'), st('vp'), st('ks'), st('vs'), st('wp'), st('ws'), st('sp'), st('ss'))
```

```python
import functools
import math

import jax
import jax.numpy as jnp
from jax import lax
from jax.experimental import pallas as pl
from jax.experimental.pallas import tpu as pltpu

F32 = jnp.float32
BF16 = jnp.bfloat16
HI = lax.Precision.HIGHEST

HEAD_DIM = 64
MOBA_BLOCK = 256
MOBA_TOPK = 3
ROPE_THETA = 10000.0
NORM_EPS = 1e-6
GN_EPS = 64e-5
NEG_INF = -1e30
DECAY_LORA = 64
ICLR_LORA = 64
GATE_LORA = 128
WKV_CHUNKS_PER_ITER = 2
LANES = 128
WKV_CHUNK = 64

VMEM_LIMIT = 56 * 1024 * 1024


def _cparams(*sem):
    return pltpu.CompilerParams(dimension_semantics=sem, vmem_limit_bytes=VMEM_LIMIT)


def _row_tile(n, target):
    t = min(n, target)
    while n % t:
        t //= 2
    return t


def _nt_dot(a, b, precision=None):
    return lax.dot_general(a, b, (((1,), (1,)), ((), ())), precision=precision, preferred_element_type=F32)


def _ffn_kernel(x_ref, g_ref, wg_ref, wu_ref, wd_ref, o_ref, h_scr, acc_scr):
    j = pl.program_id(1)

    @pl.when(j == 0)
    def _():
        x = x_ref[...]
        ms = jnp.mean(x * x, axis=-1, keepdims=True)
        h_scr[...] = (x * lax.rsqrt(ms + NORM_EPS) * g_ref[...]).astype(BF16)
        acc_scr[...] = jnp.zeros_like(acc_scr)

    h = h_scr[...]
    a = jnp.dot(h, wg_ref[...], preferred_element_type=F32)
    b = jnp.dot(h, wu_ref[...], preferred_element_type=F32)
    u = a * jax.nn.sigmoid(a) * b
    acc_scr[...] += jnp.dot(u.astype(BF16), wd_ref[...], preferred_element_type=F32)

    @pl.when(j == pl.num_programs(1) - 1)
    def _():
        o_ref[...] = x_ref[...] + 0.5 * acc_scr[...]


def ffn_half(x, g, wg, wu, wd, *, tm_target=512, tf=1408):
    n, d = x.shape
    f = wg.shape[1]
    tm = _row_tile(n, tm_target)
    assert f % tf == 0
    return pl.pallas_call(
        _ffn_kernel,
        grid=(n // tm, f // tf),
        in_specs=[
            pl.BlockSpec((tm, d), lambda i, j: (i, 0)),
            pl.BlockSpec((1, d), lambda i, j: (0, 0)),
            pl.BlockSpec((d, tf), lambda i, j: (0, j)),
            pl.BlockSpec((d, tf), lambda i, j: (0, j)),
            pl.BlockSpec((tf, d), lambda i, j: (j, 0)),
        ],
        out_specs=pl.BlockSpec((tm, d), lambda i, j: (i, 0)),
        out_shape=jax.ShapeDtypeStruct((n, d), F32),
        scratch_shapes=[pltpu.VMEM((tm, d), BF16), pltpu.VMEM((tm, d), F32)],
        compiler_params=_cparams("parallel", "arbitrary"),
        name="ffn_half",
    )(x, g.reshape(1, d), wg, wu, wd)


def _proj_kernel(x_ref, g_ref, w_ref, o_ref, h_scr, *, normalize):
    @pl.when(pl.program_id(1) == 0)
    def _():
        x = x_ref[...]
        if normalize:
            ms = jnp.mean(x * x, axis=-1, keepdims=True)
            x = x * lax.rsqrt(ms + NORM_EPS) * g_ref[...]
        h_scr[...] = x.astype(BF16)

    o_ref[...] = jnp.dot(h_scr[...], w_ref[...], preferred_element_type=F32)


def norm_proj(x, g, w, *, normalize=True, tm_target=1024, tn=256):
    n, d = x.shape
    c = w.shape[1]
    tm = _row_tile(n, tm_target)
    assert c % tn == 0
    return pl.pallas_call(
        functools.partial(_proj_kernel, normalize=normalize),
        grid=(n // tm, c // tn),
        in_specs=[
            pl.BlockSpec((tm, d), lambda i, j: (i, 0)),
            pl.BlockSpec((1, d), lambda i, j: (0, 0)),
            pl.BlockSpec((d, tn), lambda i, j: (0, j)),
        ],
        out_specs=pl.BlockSpec((tm, tn), lambda i, j: (i, j)),
        out_shape=jax.ShapeDtypeStruct((n, c), F32),
        scratch_shapes=[pltpu.VMEM((tm, d), BF16)],
        compiler_params=_cparams("parallel", "arbitrary"),
        name="norm_proj",
    )(x, g.reshape(1, d), w)


def _rms_kernel(x_ref, g_ref, o_ref):
    x = x_ref[...]
    ms = jnp.mean(x * x, axis=-1, keepdims=True)
    o_ref[...] = x * lax.rsqrt(ms + NORM_EPS) * g_ref[...]


def rms_norm_rows(x, g, *, tm_target=1024):
    n, d = x.shape
    tm = _row_tile(n, tm_target)
    return pl.pallas_call(
        _rms_kernel,
        grid=(n // tm,),
        in_specs=[pl.BlockSpec((tm, d), lambda i: (i, 0)), pl.BlockSpec((1, d), lambda i: (0, 0))],
        out_specs=pl.BlockSpec((tm, d), lambda i: (i, 0)),
        out_shape=jax.ShapeDtypeStruct((n, d), F32),
        compiler_params=_cparams("parallel"),
        name="rms_norm",
    )(x, g.reshape(1, d))


def _head_block_diag(n):
    r = lax.broadcasted_iota(jnp.int32, (n, n), 0) // HEAD_DIM
    c = lax.broadcasted_iota(jnp.int32, (n, n), 1) // HEAD_DIM
    return (r == c).astype(F32)


def _prep_kernel(*refs, has_prev_tile, with_cum, n_heads):
    if has_prev_tile:
        p_ref, prev8_ref, pprev_ref = refs[:3]
        rest = refs[3:]
    else:
        p_ref, pprev_ref = refs[:2]
        prev8_ref = None
        rest = refs[2:]
    mu_ref, w0_ref, w2_ref, a0_ref, a2_ref, g2_ref, kk_ref, ka_ref = rest[:8]
    r_out, w_out, k_out, v_out, kk_out, kka_out, g_out = rest[8:15]
    c_out = rest[15] if with_cum else None
    d_r = n_heads * HEAD_DIM
    p = p_ref[0]
    tt = p.shape[0]
    prev_row = pprev_ref[0]
    if has_prev_tile:
        prev_row = jnp.where(pl.program_id(1) == 0, prev_row, prev8_ref[0, 7:8, :])
    row = lax.broadcasted_iota(jnp.int32, p.shape, 0)
    if tt % 8 == 0:
        rolled = pltpu.roll(p, 1, 0)
    else:
        rolled = jnp.concatenate([p[tt - 1:], p[:tt - 1]], axis=0)
    p_shift = jnp.where(row == 0, prev_row, rolled)
    xm = p + (p_shift - p) * mu_ref[...]
    r = xm[:, :d_r]
    k = xm[:, d_r:2 * d_r]
    v = xm[:, 2 * d_r:3 * d_r]
    c0 = 3 * d_r
    xw = xm[:, c0:c0 + DECAY_LORA]
    xa = xm[:, c0 + DECAY_LORA:c0 + DECAY_LORA + ICLR_LORA]
    xg = xm[:, c0 + DECAY_LORA + ICLR_LORA:]
    z = -(w0_ref[...] + jnp.dot(jnp.tanh(xw), w2_ref[...], precision=HI, preferred_element_type=F32))
    softplus = jnp.maximum(z, 0.0) + jnp.log(1.0 + jnp.exp(-jnp.abs(z)))
    log_decay = -jnp.exp(-softplus - 0.5)
    if with_cum:
        ci = lax.broadcasted_iota(jnp.int32, (tt, tt), 0)
        cj = lax.broadcasted_iota(jnp.int32, (tt, tt), 1)
        tri = ((ci // WKV_CHUNK == cj // WKV_CHUNK) & (cj <= ci)).astype(F32)
        cum = jnp.dot(tri, log_decay, precision=HI, preferred_element_type=F32)
    a =jax.nn.sigmoid(a0_ref[...] + jnp.dot(xa, a2_ref[...], precision=HI, preferred_element_type=F32))
    g = jnp.dot(jax.nn.sigmoid(xg), g2_ref[...], precision=HI, preferred_element_type=F32)
    kk = k * kk_ref[...]
    ss = jnp.dot(kk * kk, _head_block_diag(d_r), precision=HI, preferred_element_type=F32)
    kk = kk / jnp.maximum(jnp.sqrt(ss), 1e-12)
    k2 = k * (1.0 + (a - 1.0) * ka_ref[...])
    kka = kk * a
    for h in range(n_heads):
        sl = slice(h * HEAD_DIM, (h + 1) * HEAD_DIM)
        r_out[0, h] = r[:, sl]
        w_out[0, h] = log_decay[:, sl]
        if with_cum:
            c_out[0, h] = cum[:, sl]
        k_out[0, h] = k2[:, sl]
        v_out[0, h] = v[:, sl]
        kk_out[0, h] = kk[:, sl]
        kka_out[0, h] = kka[:, sl]
        g_out[0, h] = g[:, sl]


def rwkv_prep(pr, p_prev, lp, *, tt_target=256):
    b, t, c = pr.shape
    d_r = lp['w0'].shape[0]
    n_heads = d_r // HEAD_DIM
    tt = _row_tile(t, tt_target)
    nt = t // tt
    has_prev = nt > 1
    with_cum = tt % WKV_CHUNK == 0
    n_out = 8 if with_cum else 7
    row = lambda x: x.reshape(1, -1)
    full = lambda shape: pl.BlockSpec(shape, lambda bi, i: (0,) * len(shape))
    in_specs = [pl.BlockSpec((1, tt, c), lambda bi, i: (bi, i, 0))]
    args = [pr]
    if has_prev:
        in_specs.append(pl.BlockSpec((1, 8, c), lambda bi, i: (bi, jnp.maximum(i * (tt // 8) - 1, 0), 0)))
        args.append(pr)
    in_specs.append(pl.BlockSpec((1, 1, c), lambda bi, i: (bi, 0, 0)))
    args.append(p_prev.reshape(b, 1, c))
    params = [row(lp['mu']), row(lp['w0']), lp['w2'], row(lp['a0']), lp['a2'], lp['g2'], row(lp['k_k']), row(lp['k_a'])]
    in_specs += [full(x.shape) for x in params]
    args += params
    hm = jax.ShapeDtypeStruct((b, n_heads, t, HEAD_DIM), F32)
    hm_spec = pl.BlockSpec((1, n_heads, tt, HEAD_DIM), lambda bi, i: (bi, 0, i, 0))
    return pl.pallas_call(
        functools.partial(_prep_kernel, has_prev_tile=has_prev, with_cum=with_cum, n_heads=n_heads),
        grid=(b, nt),
        in_specs=in_specs,
        out_specs=[hm_spec] * n_out,
        out_shape=[hm] * n_out,
        compiler_params=_cparams("parallel", "arbitrary"),
        name="rwkv_prep",
    )(*args)


def _scan_kernel(r_ref, lw_ref, k_ref, v_ref, kk_ref, kka_ref, s0_ref, y_ref, s_out_ref, s_scr, *, n_heads, tc):
    @pl.when(pl.program_id(1) == 0)
    def _():
        s_scr[...] = s0_ref[0]

    eye = (lax.broadcasted_iota(jnp.int32, (HEAD_DIM, HEAD_DIM), 0)
           == lax.broadcasted_iota(jnp.int32, (HEAD_DIM, HEAD_DIM), 1))

    def body(t, carry):
        for h in range(n_heads):
            s = s_scr[h]
            row = lambda ref: ref[0, h, pl.ds(t, 1), :]
            sa = -jnp.sum(s * row(kk_ref), axis=1, keepdims=True)
            v_col = jnp.sum(jnp.where(eye, row(v_ref), 0.0), axis=1, keepdims=True)
            s = s * jnp.exp(row(lw_ref)) + sa * row(kka_ref) + v_col * row(k_ref)
            s_scr[h] = s
            y_col = jnp.sum(s * row(r_ref), axis=1, keepdims=True)
            y_ref[0, h, pl.ds(t, 1), :] = jnp.sum(jnp.where(eye, y_col, 0.0), axis=0, keepdims=True)
        return carry

    lax.fori_loop(0, tc, body, 0)
    s_out_ref[0] = s_scr[...]


def _bdot(a, b, contract, precision=None):
    if precision is None:
        a, b = a.astype(BF16), b.astype(BF16)
    return lax.dot_general(a, b, (((contract[0],), (contract[1],)), ((0,), (0,))), precision=precision,
                           preferred_element_type=F32)


_NN, _NT, _TN = (2, 1), (2, 2), (1, 1)


def _split(x):
    hi = x.astype(BF16)
    return hi, (x - hi.astype(F32)).astype(BF16)


def _bdot_split(a, b, contract):
    (ah, al), (bh, bl) = a, b
    return _bdot(ah, bh, contract) + (_bdot(ah, bl, contract) + _bdot(al, bh, contract))


def _unit_lower_inverse(t_strict):
    n = t_strict.shape[1]
    ri = lax.broadcasted_iota(jnp.int32, (n, n), 0)
    ci = lax.broadcasted_iota(jnp.int32, (n, n), 1)
    same = lambda size: (ri // size) == (ci // size)
    mm = lambda x, y: _bdot_split(x, y, _NN)
    neg32 = jnp.where(same(8), -t_strict, 0.0)
    neg = _split(neg32)
    n2 = _split(mm(neg, neg))
    n4 = _split(mm(n2, n2))
    inv = (ri == ci).astype(F32) + neg32
    inv = inv + mm(_split(inv), n2)
    inv = inv + mm(_split(inv), n4)
    for size in (16, 32, 64):
        off = _split(jnp.where(same(size) & jnp.logical_not(same(size // 2)), t_strict, 0.0))
        inv_s = _split(inv)
        inv = inv - mm(inv_s, _split(mm(off, inv_s)))
    return inv


def _chunk_scan_kernel(r_ref, lw_ref, c_ref, k_ref, v_ref, kk_ref, kka_ref, s0_ref, y_ref, s_out_ref, s_scr,
                       *, n_heads, n_chunks):
    L = WKV_CHUNK

    @pl.when(pl.program_id(1) == 0)
    def _():
        s_scr[...] = s0_ref[0]

    ri = lax.broadcasted_iota(jnp.int32, (2 * L, 2 * L), 0)
    ci = lax.broadcasted_iota(jnp.int32, (2 * L, 2 * L), 1)
    keep = ((ri < L) & ((ci % L) < ri)) | ((ri >= L) & ((ci % L) <= (ri - L)))

    G = WKV_CHUNKS_PER_ITER

    def body(cidx, carry):
        rows = pl.ds(pl.multiple_of(cidx * (G * L), G * L), G * L)
        ld = lambda ref: ref[0, :, rows, :].reshape(n_heads * G, L, HEAD_DIM)
        r, lw, c, k, v, a, b = (ld(x) for x in (r_ref, lw_ref, c_ref, k_ref, v_ref, kk_ref, kka_ref))
        c_last = c[:, L - 1:L, :]
        inv_g = jnp.exp(-c)
        a_t = a * jnp.exp(c - lw)
        b_t = b * inv_g
        k_t = k * inv_g
        r_t = r * jnp.exp(c)
        to_end = jnp.exp(c_last - c)
        b_end = b * to_end
        k_end = k * to_end
        coef = _bdot_split(_split(jnp.concatenate([a_t, r_t], axis=1)),
                           _split(jnp.concatenate([b_t, k_t], axis=1)), _NT)
        coef = jnp.where(keep, coef, 0.0)
        t_ab, t_ak = coef[:, :L, :L], coef[:, :L, L:]
        a_rb, a_rk = coef[:, L:, :L], coef[:, L:, L:]
        m_inv = _unit_lower_inverse(t_ab)
        w_mat = _bdot(m_inv, a_t, _NN)
        u_mat = -_bdot(m_inv, _bdot(t_ak, v, _NN), _NN)
        y_intra = _bdot(a_rb, u_mat, _NN) + _bdot(a_rk, v, _NN)
        r_w = r_t - _bdot(a_rb, w_mat, _NN)
        h_mat = _bdot(u_mat, b_end, _TN) + _bdot(v, k_end, _TN)
        per_chunk = lambda x: x.reshape(n_heads, G, *x.shape[1:])
        r_w, y_intra, w_mat, b_end, h_mat, decay_end = (
            per_chunk(x) for x in (r_w, y_intra, w_mat, b_end, h_mat, jnp.exp(c_last)))
        s0 = s_scr[...]
        ys = []
        for g in range(G):
            ys.append(_bdot(r_w[:, g], s0, _NT) + y_intra[:, g])
            p = _bdot(s0, w_mat[:, g], _NT)
            s0 = s0 * decay_end[:, g] - _bdot(p, b_end[:, g], _NN) + h_mat[:, g]
        y_ref[0, :, rows, :] = jnp.concatenate(ys, axis=1)
        s_scr[...] = s0
        return carry

    lax.fori_loop(0, n_chunks // G, body, 0)
    s_out_ref[0] = s_scr[...]


def wkv_chunk_scan(r, lw, c, k, v, kk, kka, s0, *, tc_target=256):
    b, h, t, n = r.shape
    tc = _row_tile(t, tc_target)
    assert tc % (WKV_CHUNK * WKV_CHUNKS_PER_ITER) == 0 and n == WKV_CHUNK
    seq = pl.BlockSpec((1, h, tc, n), lambda bi, i: (bi, 0, i, 0))
    st = pl.BlockSpec((1, h, n, n), lambda bi, i: (bi, 0, 0, 0))
    return pl.pallas_call(
        functools.partial(_chunk_scan_kernel, n_heads=h, n_chunks=tc // WKV_CHUNK),
        grid=(b, t // tc),
        in_specs=[seq] * 7 + [st],
        out_specs=[seq, st],
        out_shape=[jax.ShapeDtypeStruct((b, h, t, n), F32), jax.ShapeDtypeStruct((b, h, n, n), F32)],
        scratch_shapes=[pltpu.VMEM((h, n, n), F32)],
        compiler_params=_cparams("parallel", "arbitrary"),
        name="wkv_chunk_scan",
    )(r, lw, c, k, v, kk, kka, s0)


def wkv_scan(r, w, k, v, kk, kka, s0, *, tc_target=256):
    b, h, t, n = r.shape
    tc = _row_tile(t, tc_target)
    seq = pl.BlockSpec((1, h, tc, n), lambda bi, c: (bi, 0, c, 0))
    st = pl.BlockSpec((1, h, n, n), lambda bi, c: (bi, 0, 0, 0))
    return pl.pallas_call(
        functools.partial(_scan_kernel, n_heads=h, tc=tc),
        grid=(b, t // tc),
        in_specs=[seq] * 6 + [st],
        out_specs=[seq, st],
        out_shape=[jax.ShapeDtypeStruct((b, h, t, n), F32), jax.ShapeDtypeStruct((b, h, n, n), F32)],
        scratch_shapes=[pltpu.VMEM((h, n, n), F32)],
        compiler_params=_cparams("parallel", "arbitrary"),
        name="wkv_scan",
    )(r, w, k, v, kk, kka, s0)


def _rope_kernel(p_ref, cos_ref, sin_ref, krow_ref, q_out, k_out=None, vt_out=None, *, n_heads, head_major):
    d_m = n_heads * HEAD_DIM
    half = HEAD_DIM // 2
    cos = cos_ref[...]
    sin = sin_ref[...]
    first_half = (lax.broadcasted_iota(jnp.int32, cos.shape, 1) % HEAD_DIM) < half

    def rot(x):
        if x.shape[0] % 8 == 0:
            fwd = pltpu.roll(x, d_m - half, 1)
            bwd = pltpu.roll(x, half, 1)
        else:
            fwd = jnp.concatenate([x[:, half:], x[:, :half]], axis=1)
            bwd = jnp.concatenate([x[:, d_m - half:], x[:, :d_m - half]], axis=1)
        return x * cos + jnp.where(first_half, fwd, bwd) * sin

    p = p_ref[0]
    q = rot(p[:, :d_m])
    k = rot(p[:, d_m:2 * d_m])
    krow_ref[0] = k
    if not head_major:
        q_out[0] = q
        return
    v = p[:, 2 * d_m:]
    for h in range(n_heads):
        sl = slice(h * HEAD_DIM, (h + 1) * HEAD_DIM)
        q_out[0, h] = q[:, sl]
        k_out[0, h] = k[:, sl].astype(BF16)
        for jb in range(p.shape[0] // MOBA_BLOCK):
            vt_out[0, h, jb] = v[jb * MOBA_BLOCK:(jb + 1) * MOBA_BLOCK, sl].T.astype(BF16)


def rope_qkv(pm, cos, sin, *, head_major, tt_target=512):
    b, t, c = pm.shape
    d_m = c // 3
    n_heads = d_m // HEAD_DIM
    tt = _row_tile(t, tt_target)
    rows_spec = pl.BlockSpec((1, tt, d_m), lambda bi, i: (bi, i, 0))
    rows = jax.ShapeDtypeStruct((b, t, d_m), F32)
    if head_major:
        assert tt % MOBA_BLOCK == 0
        hm_spec = pl.BlockSpec((1, n_heads, tt, HEAD_DIM), lambda bi, i: (bi, 0, i, 0))
        hm = lambda dt: jax.ShapeDtypeStruct((b, n_heads, t, HEAD_DIM), dt)
        nbt = tt // MOBA_BLOCK
        out_specs = [rows_spec, hm_spec, hm_spec,
                     pl.BlockSpec((1, n_heads, nbt, HEAD_DIM, MOBA_BLOCK), lambda bi, i: (bi, 0, i, 0, 0))]
        out_shape = [rows, hm(F32), hm(BF16),
                     jax.ShapeDtypeStruct((b, n_heads, t // MOBA_BLOCK, HEAD_DIM, MOBA_BLOCK), BF16)]
    else:
        out_specs = [rows_spec, rows_spec]
        out_shape = [rows, rows]
    return pl.pallas_call(
        functools.partial(_rope_kernel, n_heads=n_heads, head_major=head_major),
        grid=(b, t // tt),
        in_specs=[
            pl.BlockSpec((1, tt, c), lambda bi, i: (bi, i, 0)),
            pl.BlockSpec((tt, d_m), lambda bi, i: (i, 0)),
            pl.BlockSpec((tt, d_m), lambda bi, i: (i, 0)),
        ],
        out_specs=out_specs,
        out_shape=out_shape,
        compiler_params=_cparams("parallel", "parallel"),
        name="rope_qkv",
    )(pm, cos, sin)


def rope_tables(pos, n_heads):
    half = HEAD_DIM // 2
    inv_freq = ROPE_THETA ** (-jnp.arange(half, dtype=F32) / half)
    ang = pos.astype(F32)[:, None] * inv_freq[None, :]
    cos = jnp.cos(ang)
    sin = jnp.sin(ang)
    cos = jnp.tile(jnp.concatenate([cos, cos], axis=1), (1, n_heads))
    sin = jnp.tile(jnp.concatenate([-sin, sin], axis=1), (1, n_heads))
    return cos, sin


def _kmean_kernel(k_ref, o_ref, *, nblk):
    for j in range(nblk):
        o_ref[0, j:j + 1, :] = jnp.mean(k_ref[0, j * MOBA_BLOCK:(j + 1) * MOBA_BLOCK, :], axis=0, keepdims=True)


def block_means(k_rows):
    b, t, d_m = k_rows.shape
    nb = t // MOBA_BLOCK
    nblk = 8 if nb % 8 == 0 else nb
    return pl.pallas_call(
        functools.partial(_kmean_kernel, nblk=nblk),
        grid=(b, nb // nblk),
        in_specs=[pl.BlockSpec((1, nblk * MOBA_BLOCK, d_m), lambda bi, i: (bi, i, 0))],
        out_specs=pl.BlockSpec((1, nblk, d_m), lambda bi, i: (bi, i, 0)),
        out_shape=jax.ShapeDtypeStruct((b, nb, d_m), F32),
        compiler_params=_cparams("parallel", "parallel"),
        name="block_means",
    )(k_rows)


def _top_blocks(gate, valid, n_sel):
    ax = gate.ndim - 1
    nb = gate.shape[ax]
    blk = lax.broadcasted_iota(jnp.int32, gate.shape, ax)
    avail = valid
    sel = jnp.zeros(gate.shape, F32)
    for _ in range(n_sel):
        g = jnp.where(avail, gate, -jnp.inf)
        m = jnp.max(g, axis=ax, keepdims=True)
        first = jnp.min(jnp.where((g == m) & avail, blk, nb), axis=ax, keepdims=True)
        pick = blk == first
        sel = jnp.where(pick, 1.0, sel)
        avail = avail & jnp.logical_not(pick)
    return sel


def _top_blocks_t(gate, valid, n_sel):
    ax = gate.ndim - 2
    nb = gate.shape[ax]
    blk = lax.broadcasted_iota(jnp.int32, gate.shape, ax)
    avail = jnp.broadcast_to(valid, gate.shape)
    sel = jnp.zeros(gate.shape, F32)
    for _ in range(n_sel):
        g = jnp.where(avail, gate, -jnp.inf)
        m = jnp.max(g, axis=ax, keepdims=True)
        first = jnp.min(jnp.where((g == m) & avail, blk, nb), axis=ax, keepdims=True)
        pick = blk == first
        sel = jnp.where(pick, 1.0, sel)
        avail = avail & jnp.logical_not(pick)
    return sel


MOBA_HEADS_PER_STEP = 4
SUBLANES = 8
MOBA_BLOCKS_PER_ITER = SUBLANES // 2


def _moba_prompt_kernel(q_ref, k_ref, vt_ref, km_ref, o_ref, sel_scr, *, n_sel):
    i = pl.program_id(2)
    hb = q_ref.shape[1]
    nb = km_ref.shape[2]
    q32 = q_ref[0]
    q = (q32 * (HEAD_DIM ** -0.5 * math.log2(math.e))).astype(BF16)

    def scores_t(j0, n_blocks):
        rows = pl.ds(pl.multiple_of(j0 * MOBA_BLOCK, MOBA_BLOCK), n_blocks * MOBA_BLOCK)
        return _bdot(k_ref[0, :, rows, :], q, _NT)

    def weighted_values_t(j0, p):
        out = None
        for jb in range(p.shape[1] // MOBA_BLOCK):
            pv = _bdot(vt_ref[0, :, j0 + jb], p[:, jb * MOBA_BLOCK:(jb + 1) * MOBA_BLOCK], _NN)
            out = pv if out is None else out + pv
        return out

    key_pos = lax.broadcasted_iota(jnp.int32, (MOBA_BLOCK, MOBA_BLOCK), 0)
    q_pos = lax.broadcasted_iota(jnp.int32, (MOBA_BLOCK, MOBA_BLOCK), 1)
    s = jnp.where(key_pos <= q_pos, scores_t(i, 1), NEG_INF)
    m = jnp.max(s, axis=1, keepdims=True)
    p = jnp.exp2(s - m)
    l = jnp.sum(p, axis=1, keepdims=True)
    acc = weighted_values_t(i, p)

    if n_sel > 0:
        gate_t = _bdot(km_ref[0], q32, _NT, precision=HI)
        blk = lax.broadcasted_iota(jnp.int32, (nb, MOBA_BLOCK), 0)
        sel_scr[...] = _top_blocks_t(gate_t, blk < i, n_sel)
        kb = MOBA_BLOCKS_PER_ITER

        def body(jj, carry):
            m, l, acc = carry
            j0 = jj * kb
            sel8 = sel_scr[:, pl.ds(pl.multiple_of((jj // 2) * 8, 8), 8), :]
            chosen = jnp.where(jj % 2 == 0, sel8[:, :kb], sel8[:, kb:]) > 0.0
            s = scores_t(j0, kb).reshape(hb, kb, MOBA_BLOCK, MOBA_BLOCK)
            s = jnp.where(chosen[:, :, None, :], s, NEG_INF).reshape(hb, kb * MOBA_BLOCK, MOBA_BLOCK)
            m_new = jnp.maximum(m, jnp.max(s, axis=1, keepdims=True))
            alpha = jnp.exp2(m - m_new)
            p = jnp.exp2(s - m_new)
            l = alpha * l + jnp.sum(p, axis=1, keepdims=True)
            acc = alpha * acc + weighted_values_t(j0, p)
            return m_new, l, acc

        m, l, acc = lax.fori_loop(0, (i + kb - 1) // kb, body, (m, l, acc))

    o_t = acc / l
    for h in range(hb):
        o_ref[0, h] = o_t[h].T


def moba_prompt(q, k, vt, kmean):
    b, h, t, d = q.shape
    assert t % MOBA_BLOCK == 0
    nb = t // MOBA_BLOCK
    n_sel = min(MOBA_TOPK, nb - 1)
    nb_pad = -(-nb // 8) * 8
    kmean = jnp.pad(kmean, ((0, 0), (0, 0), (0, nb_pad - nb), (0, 0)))
    hb = MOBA_HEADS_PER_STEP
    assert h % hb == 0 and nb % MOBA_BLOCKS_PER_ITER == 0
    q_spec = pl.BlockSpec((1, hb, MOBA_BLOCK, d), lambda bi, hi, i: (bi, hi, i, 0))
    return pl.pallas_call(
        functools.partial(_moba_prompt_kernel, n_sel=n_sel),
        grid=(b, h // hb, nb),
        in_specs=[
            q_spec,
            pl.BlockSpec((1, hb, t, d), lambda bi, hi, i: (bi, hi, 0, 0)),
            pl.BlockSpec((1, hb, nb, d, MOBA_BLOCK), lambda bi, hi, i: (bi, hi, 0, 0, 0)),
            pl.BlockSpec((1, hb, nb_pad, d), lambda bi, hi, i: (bi, hi, 0, 0)),
        ],
        out_specs=q_spec,
        out_shape=jax.ShapeDtypeStruct((b, h, t, d), F32),
        scratch_shapes=[pltpu.VMEM((hb, nb_pad, MOBA_BLOCK), F32)],
        compiler_params=_cparams("parallel", "parallel", "arbitrary"),
        name="moba_prompt",
    )(q, k, vt, kmean)


PAGES_PER_STEP = 16
SAMPLE_BLOCKS_PER_STEP = 4


def _sample_kmean_kernel(pt_ref, *refs, ppb):
    pages, o_ref = refs[:-1], refs[-1]
    nblk = len(pages) // ppb
    n_heads, _, page_size = pages[0].shape[2:]
    ones = jnp.ones((n_heads, SUBLANES, page_size), BF16)
    for j in range(nblk):
        s = pages[j * ppb][0, 0]
        for q in range(1, ppb):
            s = s + pages[j * ppb + q][0, 0]
        hi, lo = _split(s)
        tot = _bdot(ones, hi, _NT) + _bdot(ones, lo, _NT)
        o_ref[0, j] = tot[:, 0, :] / (ppb * page_size)


def sample_block_means(cache_kt, layer, page_table):
    _, _, n_heads, d, page_size = cache_kt.shape
    db, n_pages = page_table.shape
    ppb = MOBA_BLOCK // page_size
    n_full = (n_pages * page_size) // MOBA_BLOCK
    pps = PAGES_PER_STEP
    assert n_pages % pps == 0 and pps % ppb == 0

    def page_spec(q):
        return pl.BlockSpec((1, 1, n_heads, d, page_size), lambda bi, g, pt: (layer, pt[bi, g * pps + q], 0, 0, 0))

    return pl.pallas_call(
        functools.partial(_sample_kmean_kernel, ppb=ppb),
        grid_spec=pltpu.PrefetchScalarGridSpec(
            num_scalar_prefetch=1,
            grid=(db, n_pages // pps),
            in_specs=[page_spec(q) for q in range(pps)],
            out_specs=pl.BlockSpec((1, pps // ppb, n_heads, d), lambda bi, g, pt: (bi, g, 0, 0)),
        ),
        out_shape=jax.ShapeDtypeStruct((db, n_full, n_heads, d), F32),
        compiler_params=_cparams("parallel", "arbitrary"),
        name="sample_block_means",
    )(page_table, *([cache_kt] * pps))


def _moba_sample_kernel(pt_ref, q_ref, kn_ref, vn_ref, km_ref, *refs, ppb, n_heads, n_sel):
    n_pages = (len(refs) - 5) // 2
    k_pages = refs[:n_pages]
    v_pages = refs[n_pages:2 * n_pages]
    o_ref = refs[2 * n_pages]
    sel_scr, m_scr, l_scr, acc_scr = refs[2 * n_pages + 1:]
    n = pl.program_id(1)
    q32 = q_ref[0]
    q = (q32 * HEAD_DIM ** -0.5).astype(BF16)
    page_size = k_pages[0].shape[4]

    @pl.when(n == 0)
    def _():
        if n_sel > 0:
            gate = _bdot(q32, km_ref[0], _NT, precision=HI)
            sel_scr[...] = _top_blocks(gate, jnp.ones(gate.shape, jnp.bool_), n_sel)
        s = _bdot(q, kn_ref[0], _NT)
        q_t = lax.broadcasted_iota(jnp.int32, s.shape, 1)
        k_t = lax.broadcasted_iota(jnp.int32, s.shape, 2)
        s = jnp.where(k_t <= q_t, s, NEG_INF)
        m = jnp.max(s, axis=2, keepdims=True)
        p = jnp.exp(s - m)
        m_scr[...] = m
        l_scr[...] = jnp.sum(p, axis=2, keepdims=True)
        acc_scr[...] = _bdot(p, vn_ref[0], _NN)

    if n_sel > 0:
        sel = sel_scr[...]
        blk = lax.broadcasted_iota(jnp.int32, sel.shape, 2)
        scores = []
        for pg in range(n_pages):
            b = n * (n_pages // ppb) + pg // ppb
            chosen = jnp.sum(jnp.where(blk == b, sel, 0.0), axis=2, keepdims=True) > 0.0
            scores.append(jnp.where(chosen, _bdot(q, k_pages[pg][0, 0], _NN), NEG_INF))
        s = jnp.concatenate(scores, axis=2)
        m = m_scr[...]
        m_new = jnp.maximum(m, jnp.max(s, axis=2, keepdims=True))
        alpha = jnp.exp(m - m_new)
        p = jnp.exp(s - m_new)
        m_scr[...] = m_new
        l_scr[...] = alpha * l_scr[...] + jnp.sum(p, axis=2, keepdims=True)
        acc = alpha * acc_scr[...]
        for pg in range(n_pages):
            acc = acc + _bdot(p[:, :, pg * page_size:(pg + 1) * page_size], v_pages[pg][0, 0], _NT)
        acc_scr[...] = acc

    @pl.when(n == pl.num_programs(1) - 1)
    def _():
        o_ref[0] = acc_scr[...] / l_scr[...]


def moba_sample(cache_kt, cache_vt, layer, page_table, q_rows, k_rows, v_rows, kmean):
    _, _, n_heads, d, page_size = cache_kt.shape
    db, t, _ = q_rows.shape
    n_pages = page_table.shape[1]
    ppb = MOBA_BLOCK // page_size
    n_full = (n_pages * page_size) // MOBA_BLOCK
    assert n_full * ppb == n_pages, "past rows inside the current block are not supported"
    n_sel = min(MOBA_TOPK, n_full)
    tp = -(-t // SUBLANES) * SUBLANES
    hm = lambda x: jnp.pad(_to_heads(x), ((0, 0), (0, 0), (0, tp - t), (0, 0)))
    new_spec = pl.BlockSpec((1, n_heads, tp, d), lambda bi, n, pt: (bi, 0, 0, 0))

    pps = SAMPLE_BLOCKS_PER_STEP * ppb
    assert n_pages % pps == 0

    def page_spec(q):
        return pl.BlockSpec((1, 1, n_heads, d, page_size), lambda bi, n, pt: (layer, pt[bi, n * pps + q], 0, 0, 0))

    o = pl.pallas_call(
        functools.partial(_moba_sample_kernel, ppb=ppb, n_heads=n_heads, n_sel=n_sel),
        grid_spec=pltpu.PrefetchScalarGridSpec(
            num_scalar_prefetch=1,
            grid=(db, n_pages // pps),
            in_specs=[new_spec, new_spec, new_spec,
                      pl.BlockSpec((1, n_heads, n_full, d), lambda bi, n, pt: (bi, 0, 0, 0))]
            + [page_spec(q) for q in range(pps)] * 2,
            out_specs=new_spec,
            scratch_shapes=[
                pltpu.VMEM((n_heads, tp, n_full), F32),
                pltpu.VMEM((n_heads, tp, 1), F32),
                pltpu.VMEM((n_heads, tp, 1), F32),
                pltpu.VMEM((n_heads, tp, d), F32),
            ],
        ),
        out_shape=jax.ShapeDtypeStruct((db, n_heads, tp, d), F32),
        compiler_params=_cparams("parallel", "arbitrary"),
        name="moba_sample",
    )(page_table, hm(q_rows), hm(k_rows), hm(v_rows), kmean, *([cache_kt] * pps), *([cache_vt] * pps))
    return o[:, :, :t]


def _mix_out_kernel(y_ref, r_ref, k_ref, v_ref, g_ref, om_ref, x_ref, wo_ref, lnw_ref, lnb_ref, rk_ref, o_ref,
                    *, h_rwkv, h_moba):
    acc = x_ref[0]
    for h in range(h_rwkv):
        y = y_ref[0, h]
        mean = jnp.mean(y, axis=-1, keepdims=True)
        yc = y - mean
        var = jnp.mean(yc * yc, axis=-1, keepdims=True)
        yn = yc * lax.rsqrt(var + GN_EPS) * lnw_ref[h:h + 1, :] + lnb_ref[h:h + 1, :]
        bonus = jnp.sum(r_ref[0, h] * k_ref[0, h] * rk_ref[h:h + 1, :], axis=-1, keepdims=True) * v_ref[0, h]
        o = ((yn + bonus) * g_ref[0, h]).astype(BF16)
        acc = acc + jnp.dot(o, wo_ref[h * HEAD_DIM:(h + 1) * HEAD_DIM, :], preferred_element_type=F32)
    for h in range(h_moba):
        row0 = (h_rwkv + h) * HEAD_DIM
        acc = acc + jnp.dot(om_ref[0, h].astype(BF16), wo_ref[row0:row0 + HEAD_DIM, :], preferred_element_type=F32)
    o_ref[0] = acc


def mix_out(y, r, k, v, g, o_moba, x, w_out, ln_w, ln_b, r_k, *, tt_target=512):
    b, h_rwkv, t, n = y.shape
    h_moba = o_moba.shape[1]
    d = x.shape[2]
    tt = _row_tile(t, tt_target)
    hm_r = pl.BlockSpec((1, h_rwkv, tt, n), lambda bi, i: (bi, 0, i, 0))
    hm_m = pl.BlockSpec((1, h_moba, tt, n), lambda bi, i: (bi, 0, i, 0))
    full = lambda shape: pl.BlockSpec(shape, lambda bi, i: (0,) * len(shape))
    params = [w_out, ln_w.reshape(h_rwkv, n), ln_b.reshape(h_rwkv, n), r_k.reshape(h_rwkv, n)]
    return pl.pallas_call(
        functools.partial(_mix_out_kernel, h_rwkv=h_rwkv, h_moba=h_moba),
        grid=(b, t // tt),
        in_specs=[hm_r] * 5 + [hm_m, pl.BlockSpec((1, tt, d), lambda bi, i: (bi, i, 0))] + [full(p.shape) for p in params],
        out_specs=pl.BlockSpec((1, tt, d), lambda bi, i: (bi, i, 0)),
        out_shape=jax.ShapeDtypeStruct((b, t, d), F32),
        compiler_params=_cparams("parallel", "parallel"),
        name="mix_out",
    )(y, r, k, v, g, o_moba, x, *params)


def _to_heads(x_rows):
    b, t, dm = x_rows.shape
    return x_rows.reshape(b, t, dm // HEAD_DIM, HEAD_DIM).transpose(0, 2, 1, 3)


def _mixer_front(y, lw, p_prev_rows, wkv_prev, cos, sin, *, head_major):
    b, t, d = y.shape
    flat = y.reshape(b * t, d)
    pr = norm_proj(flat, lw['mix_norm'], lw['w_in_r']).reshape(b, t, -1)
    pm = norm_proj(flat, lw['mix_norm'], lw['w_in_m']).reshape(b, t, -1)
    shift = rms_norm_rows(y[:, -1], lw['mix_norm'])
    r, log_w, k, v, kk, kka, g, *cum = rwkv_prep(pr, p_prev_rows, lw)
    if cum:
        y_wkv, wkv_new = wkv_chunk_scan(r, log_w, cum[0], k, v, kk, kka, wkv_prev)
    else:
        y_wkv, wkv_new = wkv_scan(r, log_w, k, v, kk, kka, wkv_prev)
    k_rows, *moba_in = rope_qkv(pm, cos, sin, head_major=head_major)
    d_m = k_rows.shape[2]
    v_rows = pm[:, :, 2 * d_m:]
    return dict(r=r, k=k, v=v, g=g, y=y_wkv, wkv=wkv_new, shift=shift, k_rows=k_rows, v_rows=v_rows, moba_in=moba_in)


@jax.jit
def kernel(x_prompt, x_sample, cache_k, cache_v, state_wkv, state_shift, page_table, ffn1_norm, ffn1_w_gate,
           ffn1_w_up, ffn1_w_down, mix_norm, w_in, w_out, rwkv_mu, rwkv_w0, rwkv_w2, rwkv_a0, rwkv_a2, rwkv_g2,
           rwkv_k_k, rwkv_k_a, rwkv_r_k, rwkv_ln_w, rwkv_ln_b, ffn2_norm, ffn2_w_gate, ffn2_w_up, ffn2_w_down,
           final_norm):
    b, t, d = x_prompt.shape
    db, ts, _ = x_sample.shape
    depth = w_in.shape[0]
    d_r = rwkv_w0.shape[1]
    h_rwkv = d_r // HEAD_DIM
    c_r = rwkv_mu.shape[1]
    d_m = (w_in.shape[2] - c_r) // 3
    h_moba = d_m // HEAD_DIM
    past_len = page_table.shape[1] * cache_k.shape[2]
    ck = cache_k.transpose(0, 1, 3, 4, 2)
    cv = cache_v.transpose(0, 1, 3, 4, 2)
    cos_p, sin_p = rope_tables(jnp.arange(t), h_moba)
    cos_s, sin_s = rope_tables(past_len + jnp.arange(ts), h_moba)

    yp, ys = x_prompt, x_sample
    outs = {n: [] for n in ('kp', 'vp', 'ks', 'vs', 'wp', 'ws', 'sp', 'ss')}
    for l in range(depth):
        lw = {'mix_norm': mix_norm[l], 'w_in_r': w_in[l, :, :c_r].astype(BF16), 'w_in_m': w_in[l, :, c_r:].astype(BF16),
              'mu': rwkv_mu[l], 'w0': rwkv_w0[l], 'w2': rwkv_w2[l], 'a0': rwkv_a0[l], 'a2': rwkv_a2[l],
              'g2': rwkv_g2[l], 'k_k': rwkv_k_k[l], 'k_a': rwkv_k_a[l]}
        wo = w_out[l].astype(BF16)
        f1 = (ffn1_norm[l], ffn1_w_gate[l].astype(BF16), ffn1_w_up[l].astype(BF16), ffn1_w_down[l].astype(BF16))
        f2 = (ffn2_norm[l], ffn2_w_gate[l].astype(BF16), ffn2_w_up[l].astype(BF16), ffn2_w_down[l].astype(BF16))

        yp = ffn_half(yp.reshape(b * t, d), *f1).reshape(b, t, d)
        ys = ffn_half(ys.reshape(db * ts, d), *f1).reshape(db, ts, d)

        fp = _mixer_front(yp, lw, jnp.zeros((b, c_r), F32), jnp.zeros((b, h_rwkv, HEAD_DIM, HEAD_DIM), F32),
                          cos_p, sin_p, head_major=True)
        p_prev_s = norm_proj(state_shift[l], lw['mix_norm'], lw['w_in_r'], normalize=False)
        fs = _mixer_front(ys, lw, p_prev_s, state_wkv[l], cos_s, sin_s, head_major=False)

        kmean_p = _to_heads(block_means(fp['k_rows']))
        om_p = moba_prompt(*fp['moba_in'], kmean_p)
        kmean_s = sample_block_means(ck, l, page_table).transpose(0, 2, 1, 3)
        om_s = moba_sample(ck, cv, l, page_table, fs['moba_in'][0], fs['k_rows'], fs['v_rows'], kmean_s)

        mix = lambda f, om, y: mix_out(f['y'], f['r'], f['k'], f['v'], f['g'], om, y, wo,
                                       rwkv_ln_w[l], rwkv_ln_b[l], rwkv_r_k[l])
        yp = mix(fp, om_p, yp)
        ys = mix(fs, om_s, ys)

        yp = ffn_half(yp.reshape(b * t, d), *f2).reshape(b, t, d)
        ys = ffn_half(ys.reshape(db * ts, d), *f2).reshape(db, ts, d)

        outs['kp'].append(fp['k_rows'].reshape(b, t, h_moba, HEAD_DIM))
        outs['vp'].append(fp['v_rows'].reshape(b, t, h_moba, HEAD_DIM))
        outs['ks'].append(fs['k_rows'].reshape(db, ts, h_moba, HEAD_DIM))
        outs['vs'].append(fs['v_rows'].reshape(db, ts, h_moba, HEAD_DIM))
        outs['wp'].append(fp['wkv'])
        outs['ws'].append(fs['wkv'])
        outs['sp'].append(fp['shift'])
        outs['ss'].append(fs['shift'])

    y_prompt = rms_norm_rows(yp.reshape(b * t, d), final_norm).reshape(b, t, d)
    y_sample = rms_norm_rows(ys.reshape(db * ts, d), final_norm).reshape(db, ts, d)
    st = lambda n: jnp.stack(outs[n])
    return (y_prompt, y_sample, st('kp'), st('vp'), st('ks'), st('vs'), st('wp'), st('ws'), st('sp'), st('ss'))
```

```python
import functools
import math

import jax
import jax.numpy as jnp
from jax import lax
from jax.experimental import pallas as pl
from jax.experimental.pallas import tpu as pltpu

F32 = jnp.float32
BF16 = jnp.bfloat16
HI = lax.Precision.HIGHEST

HEAD_DIM = 64
MOBA_BLOCK = 256
MOBA_TOPK = 3
ROPE_THETA = 10000.0
NORM_EPS = 1e-6
GN_EPS = 64e-5
NEG_INF = -1e30
DECAY_LORA = 64
ICLR_LORA = 64
GATE_LORA = 128
WKV_CHUNKS_PER_ITER = 4
LANES = 128
WKV_CHUNK = 64

VMEM_LIMIT = 56 * 1024 * 1024


def _cparams(*sem):
    return pltpu.CompilerParams(dimension_semantics=sem, vmem_limit_bytes=VMEM_LIMIT)


def _row_tile(n, target):
    t = min(n, target)
    while n % t:
        t //= 2
    return t


def _nt_dot(a, b, precision=None):
    return lax.dot_general(a, b, (((1,), (1,)), ((), ())), precision=precision, preferred_element_type=F32)


def _ffn_kernel(x_ref, g_ref, wg_ref, wu_ref, wd_ref, o_ref, h_scr, acc_scr):
    j = pl.program_id(1)

    @pl.when(j == 0)
    def _():
        x = x_ref[...]
        ms = jnp.mean(x * x, axis=-1, keepdims=True)
        h_scr[...] = (x * lax.rsqrt(ms + NORM_EPS) * g_ref[...]).astype(BF16)
        acc_scr[...] = jnp.zeros_like(acc_scr)

    h = h_scr[...]
    a = jnp.dot(h, wg_ref[...], preferred_element_type=F32)
    b = jnp.dot(h, wu_ref[...], preferred_element_type=F32)
    u = a * jax.nn.sigmoid(a) * b
    acc_scr[...] += jnp.dot(u.astype(BF16), wd_ref[...], preferred_element_type=F32)

    @pl.when(j == pl.num_programs(1) - 1)
    def _():
        o_ref[...] = x_ref[...] + 0.5 * acc_scr[...]


def ffn_half(x, g, wg, wu, wd, *, tm_target=512, tf=1408):
    n, d = x.shape
    f = wg.shape[1]
    tm = _row_tile(n, tm_target)
    assert f % tf == 0
    return pl.pallas_call(
        _ffn_kernel,
        grid=(n // tm, f // tf),
        in_specs=[
            pl.BlockSpec((tm, d), lambda i, j: (i, 0)),
            pl.BlockSpec((1, d), lambda i, j: (0, 0)),
            pl.BlockSpec((d, tf), lambda i, j: (0, j)),
            pl.BlockSpec((d, tf), lambda i, j: (0, j)),
            pl.BlockSpec((tf, d), lambda i, j: (j, 0)),
        ],
        out_specs=pl.BlockSpec((tm, d), lambda i, j: (i, 0)),
        out_shape=jax.ShapeDtypeStruct((n, d), F32),
        scratch_shapes=[pltpu.VMEM((tm, d), BF16), pltpu.VMEM((tm, d), F32)],
        compiler_params=_cparams("parallel", "arbitrary"),
        name="ffn_half",
    )(x, g.reshape(1, d), wg, wu, wd)


def _proj_kernel(x_ref, g_ref, w_ref, o_ref, h_scr, *, normalize):
    @pl.when(pl.program_id(1) == 0)
    def _():
        x = x_ref[...]
        if normalize:
            ms = jnp.mean(x * x, axis=-1, keepdims=True)
            x = x * lax.rsqrt(ms + NORM_EPS) * g_ref[...]
        h_scr[...] = x.astype(BF16)

    o_ref[...] = jnp.dot(h_scr[...], w_ref[...], preferred_element_type=F32)


def norm_proj(x, g, w, *, normalize=True, tm_target=1024):
    n, d = x.shape
    c = w.shape[1]
    tm = _row_tile(n, tm_target)
    tn = c
    return pl.pallas_call(
        functools.partial(_proj_kernel, normalize=normalize),
        grid=(n // tm, c // tn),
        in_specs=[
            pl.BlockSpec((tm, d), lambda i, j: (i, 0)),
            pl.BlockSpec((1, d), lambda i, j: (0, 0)),
            pl.BlockSpec((d, tn), lambda i, j: (0, j)),
        ],
        out_specs=pl.BlockSpec((tm, tn), lambda i, j: (i, j)),
        out_shape=jax.ShapeDtypeStruct((n, c), F32),
        scratch_shapes=[pltpu.VMEM((tm, d), BF16)],
        compiler_params=_cparams("parallel", "arbitrary"),
        name="norm_proj",
    )(x, g.reshape(1, d), w)


def _rms_kernel(x_ref, g_ref, o_ref):
    x = x_ref[...]
    ms = jnp.mean(x * x, axis=-1, keepdims=True)
    o_ref[...] = x * lax.rsqrt(ms + NORM_EPS) * g_ref[...]


def rms_norm_rows(x, g, *, tm_target=1024):
    n, d = x.shape
    tm = _row_tile(n, tm_target)
    return pl.pallas_call(
        _rms_kernel,
        grid=(n // tm,),
        in_specs=[pl.BlockSpec((tm, d), lambda i: (i, 0)), pl.BlockSpec((1, d), lambda i: (0, 0))],
        out_specs=pl.BlockSpec((tm, d), lambda i: (i, 0)),
        out_shape=jax.ShapeDtypeStruct((n, d), F32),
        compiler_params=_cparams("parallel"),
        name="rms_norm",
    )(x, g.reshape(1, d))


def _head_block_diag(n):
    r = lax.broadcasted_iota(jnp.int32, (n, n), 0) // HEAD_DIM
    c = lax.broadcasted_iota(jnp.int32, (n, n), 1) // HEAD_DIM
    return (r == c).astype(F32)


def _prep_kernel(*refs, has_prev_tile, with_cum, n_heads):
    if has_prev_tile:
        p_ref, prev8_ref, pprev_ref = refs[:3]
        rest = refs[3:]
    else:
        p_ref, pprev_ref = refs[:2]
        prev8_ref = None
        rest = refs[2:]
    mu_ref, w0_ref, w2_ref, a0_ref, a2_ref, g2_ref, kk_ref, ka_ref = rest[:8]
    r_out, w_out, k_out, v_out, kk_out, kka_out, g_out = rest[8:15]
    c_out = rest[15] if with_cum else None
    d_r = n_heads * HEAD_DIM
    p = p_ref[0]
    tt = p.shape[0]
    prev_row = pprev_ref[0]
    if has_prev_tile:
        prev_row = jnp.where(pl.program_id(1) == 0, prev_row, prev8_ref[0, 7:8, :])
    row = lax.broadcasted_iota(jnp.int32, p.shape, 0)
    if tt % 8 == 0:
        rolled = pltpu.roll(p, 1, 0)
    else:
        rolled = jnp.concatenate([p[tt - 1:], p[:tt - 1]], axis=0)
    p_shift = jnp.where(row == 0, prev_row, rolled)
    xm = p + (p_shift - p) * mu_ref[...]
    r = xm[:, :d_r]
    k = xm[:, d_r:2 * d_r]
    v = xm[:, 2 * d_r:3 * d_r]
    c0 = 3 * d_r
    xw = xm[:, c0:c0 + DECAY_LORA]
    xa = xm[:, c0 + DECAY_LORA:c0 + DECAY_LORA + ICLR_LORA]
    xg = xm[:, c0 + DECAY_LORA + ICLR_LORA:]
    z = -(w0_ref[...] + _dot_split(jnp.tanh(xw), w2_ref[...]))
    softplus = jnp.maximum(z, 0.0) + jnp.log(1.0 + jnp.exp(-jnp.abs(z)))
    log_decay = -jnp.exp(-softplus - 0.5)
    if with_cum:
        ci = lax.broadcasted_iota(jnp.int32, (tt, tt), 0)
        cj = lax.broadcasted_iota(jnp.int32, (tt, tt), 1)
        tri = ((ci // WKV_CHUNK == cj // WKV_CHUNK) & (cj <= ci)).astype(BF16)
        cum = sum(jnp.dot(tri, piece, preferred_element_type=F32) for piece in _split3(log_decay))
    a = jax.nn.sigmoid(a0_ref[...] + _dot_split(xa, a2_ref[...]))
    g = _dot_split(jax.nn.sigmoid(xg), g2_ref[...])
    kk = k * kk_ref[...]
    pair_diag = _head_block_diag(LANES).astype(BF16)
    sq = kk * kk
    ss = jnp.concatenate(
        [sum(jnp.dot(piece, pair_diag, preferred_element_type=F32) for piece in _split3(sq[:, c:c + LANES]))
         for c in range(0, d_r, LANES)], axis=1)
    kk = kk / jnp.maximum(jnp.sqrt(ss), 1e-12)
    k2 = k * (1.0 + (a - 1.0) * ka_ref[...])
    kka = kk * a
    for h in range(n_heads):
        sl = slice(h * HEAD_DIM, (h + 1) * HEAD_DIM)
        r_out[0, h] = r[:, sl]
        w_out[0, h] = log_decay[:, sl]
        if with_cum:
            c_out[0, h] = cum[:, sl]
        k_out[0, h] = k2[:, sl]
        v_out[0, h] = v[:, sl]
        kk_out[0, h] = kk[:, sl]
        kka_out[0, h] = kka[:, sl]
        g_out[0, h] = g[:, sl]


def rwkv_prep(pr, p_prev, lp, *, tt_target=256):
    b, t, c = pr.shape
    d_r = lp['w0'].shape[0]
    n_heads = d_r // HEAD_DIM
    tt = _row_tile(t, tt_target)
    nt = t // tt
    has_prev = nt > 1
    with_cum = tt % WKV_CHUNK == 0
    n_out = 8 if with_cum else 7
    row = lambda x: x.reshape(1, -1)
    full = lambda shape: pl.BlockSpec(shape, lambda bi, i: (0,) * len(shape))
    in_specs = [pl.BlockSpec((1, tt, c), lambda bi, i: (bi, i, 0))]
    args = [pr]
    if has_prev:
        in_specs.append(pl.BlockSpec((1, 8, c), lambda bi, i: (bi, jnp.maximum(i * (tt // 8) - 1, 0), 0)))
        args.append(pr)
    in_specs.append(pl.BlockSpec((1, 1, c), lambda bi, i: (bi, 0, 0)))
    args.append(p_prev.reshape(b, 1, c))
    params = [row(lp['mu']), row(lp['w0']), lp['w2'], row(lp['a0']), lp['a2'], lp['g2'], row(lp['k_k']), row(lp['k_a'])]
    in_specs += [full(x.shape) for x in params]
    args += params
    hm = jax.ShapeDtypeStruct((b, n_heads, t, HEAD_DIM), F32)
    hm_spec = pl.BlockSpec((1, n_heads, tt, HEAD_DIM), lambda bi, i: (bi, 0, i, 0))
    return pl.pallas_call(
        functools.partial(_prep_kernel, has_prev_tile=has_prev, with_cum=with_cum, n_heads=n_heads),
        grid=(b, nt),
        in_specs=in_specs,
        out_specs=[hm_spec] * n_out,
        out_shape=[hm] * n_out,
        compiler_params=_cparams("parallel", "arbitrary"),
        name="rwkv_prep",
    )(*args)


def _scan_kernel(r_ref, lw_ref, k_ref, v_ref, kk_ref, kka_ref, s0_ref, y_ref, s_out_ref, s_scr, *, n_heads, tc):
    @pl.when(pl.program_id(1) == 0)
    def _():
        s_scr[...] = s0_ref[0]

    eye = (lax.broadcasted_iota(jnp.int32, (HEAD_DIM, HEAD_DIM), 0)
           == lax.broadcasted_iota(jnp.int32, (HEAD_DIM, HEAD_DIM), 1))

    def body(t, carry):
        for h in range(n_heads):
            s = s_scr[h]
            row = lambda ref: ref[0, h, pl.ds(t, 1), :]
            sa = -jnp.sum(s * row(kk_ref), axis=1, keepdims=True)
            v_col = jnp.sum(jnp.where(eye, row(v_ref), 0.0), axis=1, keepdims=True)
            s = s * jnp.exp(row(lw_ref)) + sa * row(kka_ref) + v_col * row(k_ref)
            s_scr[h] = s
            y_col = jnp.sum(s * row(r_ref), axis=1, keepdims=True)
            y_ref[0, h, pl.ds(t, 1), :] = jnp.sum(jnp.where(eye, y_col, 0.0), axis=0, keepdims=True)
        return carry

    lax.fori_loop(0, tc, body, 0)
    s_out_ref[0] = s_scr[...]


def _bdot(a, b, contract, precision=None):
    if precision is None:
        a, b = a.astype(BF16), b.astype(BF16)
    return lax.dot_general(a, b, (((contract[0],), (contract[1],)), ((0,), (0,))), precision=precision,
                           preferred_element_type=F32)


_NN, _NT, _TN = (2, 1), (2, 2), (1, 1)


def _split(x):
    hi = x.astype(BF16)
    return hi, (x - hi.astype(F32)).astype(BF16)


def _split3(x):
    hi = x.astype(BF16)
    rest = x - hi.astype(F32)
    mid = rest.astype(BF16)
    return hi, mid, (rest - mid.astype(F32)).astype(BF16)


def _dot_split(a, b):
    (ah, al), (bh, bl) = _split(a), _split(b)
    dot = lambda x, y: jnp.dot(x, y, preferred_element_type=F32)
    return dot(ah, bh) + (dot(ah, bl) + dot(al, bh))


def _bdot_split(a, b, contract):
    (ah, al), (bh, bl) = a, b
    return _bdot(ah, bh, contract) + (_bdot(ah, bl, contract) + _bdot(al, bh, contract))


def _unit_lower_inverse(t_strict):
    n = t_strict.shape[1]
    ri = lax.broadcasted_iota(jnp.int32, (n, n), 0)
    ci = lax.broadcasted_iota(jnp.int32, (n, n), 1)
    same = lambda size: (ri // size) == (ci // size)
    mm = lambda x, y: _bdot_split(x, y, _NN)
    neg32 = jnp.where(same(8), -t_strict, 0.0)
    neg = _split(neg32)
    n2 = _split(mm(neg, neg))
    n4 = _split(mm(n2, n2))
    inv = (ri == ci).astype(F32) + neg32
    inv = inv + mm(_split(inv), n2)
    inv = inv + mm(_split(inv), n4)
    for size in (16, 32, 64):
        off = _split(jnp.where(same(size) & jnp.logical_not(same(size // 2)), t_strict, 0.0))
        inv_s = _split(inv)
        inv = inv - mm(inv_s, _split(mm(off, inv_s)))
    return inv


def _chunk_scan_kernel(r_ref, lw_ref, c_ref, k_ref, v_ref, kk_ref, kka_ref, s0_ref, y_ref, s_out_ref, s_scr,
                       *, n_heads, n_chunks):
    L = WKV_CHUNK

    @pl.when(pl.program_id(1) == 0)
    def _():
        s_scr[...] = s0_ref[0]

    ri = lax.broadcasted_iota(jnp.int32, (2 * L, 2 * L), 0)
    ci = lax.broadcasted_iota(jnp.int32, (2 * L, 2 * L), 1)
    keep = ((ri < L) & ((ci % L) < ri)) | ((ri >= L) & ((ci % L) <= (ri - L)))

    G = WKV_CHUNKS_PER_ITER

    def body(cidx, carry):
        rows = pl.ds(pl.multiple_of(cidx * (G * L), G * L), G * L)
        ld = lambda ref: ref[0, :, rows, :].reshape(n_heads * G, L, HEAD_DIM)
        r, lw, c, k, v, a, b = (ld(x) for x in (r_ref, lw_ref, c_ref, k_ref, v_ref, kk_ref, kka_ref))
        c_last = c[:, L - 1:L, :]
        inv_g = jnp.exp(-c)
        a_t = a * jnp.exp(c - lw)
        b_t = b * inv_g
        k_t = k * inv_g
        r_t = r * jnp.exp(c)
        to_end = jnp.exp(c_last - c)
        b_end = b * to_end
        k_end = k * to_end
        coef = _bdot_split(_split(jnp.concatenate([a_t, r_t], axis=1)),
                           _split(jnp.concatenate([b_t, k_t], axis=1)), _NT)
        coef = jnp.where(keep, coef, 0.0)
        t_ab, t_ak = coef[:, :L, :L], coef[:, :L, L:]
        a_rb, a_rk = coef[:, L:, :L], coef[:, L:, L:]
        m_inv = _unit_lower_inverse(t_ab)
        w_mat = _bdot(m_inv, a_t, _NN)
        u_mat = -_bdot(m_inv, _bdot(t_ak, v, _NN), _NN)
        y_intra = _bdot(a_rb, u_mat, _NN) + _bdot(a_rk, v, _NN)
        r_w = r_t - _bdot(a_rb, w_mat, _NN)
        h_mat = _bdot(u_mat, b_end, _TN) + _bdot(v, k_end, _TN)
        per_chunk = lambda x: x.reshape(n_heads, G, *x.shape[1:])
        r_w, y_intra, w_mat, b_end, h_mat, decay_end = (
            per_chunk(x) for x in (r_w, y_intra, w_mat, b_end, h_mat, jnp.exp(c_last)))
        s0 = s_scr[...]
        ys = []
        for g in range(G):
            ys.append(_bdot(r_w[:, g], s0, _NT) + y_intra[:, g])
            p = _bdot(s0, w_mat[:, g], _NT)
            s0 = s0 * decay_end[:, g] - _bdot(p, b_end[:, g], _NN) + h_mat[:, g]
        y_ref[0, :, rows, :] = jnp.concatenate(ys, axis=1)
        s_scr[...] = s0
        return carry

    lax.fori_loop(0, n_chunks // G, body, 0)
    s_out_ref[0] = s_scr[...]


def wkv_chunk_scan(r, lw, c, k, v, kk, kka, s0, *, tc_target=256):
    b, h, t, n = r.shape
    tc = _row_tile(t, tc_target)
    assert tc % (WKV_CHUNK * WKV_CHUNKS_PER_ITER) == 0 and n == WKV_CHUNK
    seq = pl.BlockSpec((1, h, tc, n), lambda bi, i: (bi, 0, i, 0))
    st = pl.BlockSpec((1, h, n, n), lambda bi, i: (bi, 0, 0, 0))
    return pl.pallas_call(
        functools.partial(_chunk_scan_kernel, n_heads=h, n_chunks=tc // WKV_CHUNK),
        grid=(b, t // tc),
        in_specs=[seq] * 7 + [st],
        out_specs=[seq, st],
        out_shape=[jax.ShapeDtypeStruct((b, h, t, n), F32), jax.ShapeDtypeStruct((b, h, n, n), F32)],
        scratch_shapes=[pltpu.VMEM((h, n, n), F32)],
        compiler_params=_cparams("parallel", "arbitrary"),
        name="wkv_chunk_scan",
    )(r, lw, c, k, v, kk, kka, s0)


def wkv_scan(r, w, k, v, kk, kka, s0, *, tc_target=256):
    b, h, t, n = r.shape
    tc = _row_tile(t, tc_target)
    seq = pl.BlockSpec((1, h, tc, n), lambda bi, c: (bi, 0, c, 0))
    st = pl.BlockSpec((1, h, n, n), lambda bi, c: (bi, 0, 0, 0))
    return pl.pallas_call(
        functools.partial(_scan_kernel, n_heads=h, tc=tc),
        grid=(b, t // tc),
        in_specs=[seq] * 6 + [st],
        out_specs=[seq, st],
        out_shape=[jax.ShapeDtypeStruct((b, h, t, n), F32), jax.ShapeDtypeStruct((b, h, n, n), F32)],
        scratch_shapes=[pltpu.VMEM((h, n, n), F32)],
        compiler_params=_cparams("parallel", "arbitrary"),
        name="wkv_scan",
    )(r, w, k, v, kk, kka, s0)


def _rope_kernel(p_ref, cos_ref, sin_ref, krow_ref, q_out, k_out=None, vt_out=None, *, n_heads, head_major):
    d_m = n_heads * HEAD_DIM
    half = HEAD_DIM // 2
    cos = cos_ref[...]
    sin = sin_ref[...]
    first_half = (lax.broadcasted_iota(jnp.int32, cos.shape, 1) % HEAD_DIM) < half

    def rot(x):
        if x.shape[0] % 8 == 0:
            fwd = pltpu.roll(x, d_m - half, 1)
            bwd = pltpu.roll(x, half, 1)
        else:
            fwd = jnp.concatenate([x[:, half:], x[:, :half]], axis=1)
            bwd = jnp.concatenate([x[:, d_m - half:], x[:, :d_m - half]], axis=1)
        return x * cos + jnp.where(first_half, fwd, bwd) * sin

    p = p_ref[0]
    q = rot(p[:, :d_m])
    k = rot(p[:, d_m:2 * d_m])
    krow_ref[0] = k
    if not head_major:
        q_out[0] = q
        return
    v = p[:, 2 * d_m:]
    for h in range(n_heads):
        sl = slice(h * HEAD_DIM, (h + 1) * HEAD_DIM)
        q_out[0, h] = q[:, sl]
        k_out[0, h] = k[:, sl].astype(BF16)
        for jb in range(p.shape[0] // MOBA_BLOCK):
            vt = v[jb * MOBA_BLOCK:(jb + 1) * MOBA_BLOCK, sl].T
            ones = jnp.ones((VT_ROWS - HEAD_DIM, MOBA_BLOCK), F32)
            vt_out[0, h, jb] = jnp.concatenate([vt, ones], axis=0).astype(BF16)


def rope_qkv(pm, cos, sin, *, head_major, tt_target=512):
    b, t, c = pm.shape
    d_m = c // 3
    n_heads = d_m // HEAD_DIM
    tt = _row_tile(t, tt_target)
    rows_spec = pl.BlockSpec((1, tt, d_m), lambda bi, i: (bi, i, 0))
    rows = jax.ShapeDtypeStruct((b, t, d_m), F32)
    if head_major:
        assert tt % MOBA_BLOCK == 0
        hm_spec = pl.BlockSpec((1, n_heads, tt, HEAD_DIM), lambda bi, i: (bi, 0, i, 0))
        hm = lambda dt: jax.ShapeDtypeStruct((b, n_heads, t, HEAD_DIM), dt)
        nbt = tt // MOBA_BLOCK
        out_specs = [rows_spec, hm_spec, hm_spec,
                     pl.BlockSpec((1, n_heads, nbt, VT_ROWS, MOBA_BLOCK), lambda bi, i: (bi, 0, i, 0, 0))]
        out_shape = [rows, hm(F32), hm(BF16),
                     jax.ShapeDtypeStruct((b, n_heads, t // MOBA_BLOCK, VT_ROWS, MOBA_BLOCK), BF16)]
    else:
        out_specs = [rows_spec, rows_spec]
        out_shape = [rows, rows]
    return pl.pallas_call(
        functools.partial(_rope_kernel, n_heads=n_heads, head_major=head_major),
        grid=(b, t // tt),
        in_specs=[
            pl.BlockSpec((1, tt, c), lambda bi, i: (bi, i, 0)),
            pl.BlockSpec((tt, d_m), lambda bi, i: (i, 0)),
            pl.BlockSpec((tt, d_m), lambda bi, i: (i, 0)),
        ],
        out_specs=out_specs,
        out_shape=out_shape,
        compiler_params=_cparams("parallel", "parallel"),
        name="rope_qkv",
    )(pm, cos, sin)


def rope_tables(pos, n_heads):
    half = HEAD_DIM // 2
    inv_freq = ROPE_THETA ** (-jnp.arange(half, dtype=F32) / half)
    ang = pos.astype(F32)[:, None] * inv_freq[None, :]
    cos = jnp.cos(ang)
    sin = jnp.sin(ang)
    cos = jnp.tile(jnp.concatenate([cos, cos], axis=1), (1, n_heads))
    sin = jnp.tile(jnp.concatenate([-sin, sin], axis=1), (1, n_heads))
    return cos, sin


def _kmean_kernel(k_ref, o_ref, *, nblk):
    for j in range(nblk):
        o_ref[0, j:j + 1, :] = jnp.mean(k_ref[0, j * MOBA_BLOCK:(j + 1) * MOBA_BLOCK, :], axis=0, keepdims=True)


def block_means(k_rows):
    b, t, d_m = k_rows.shape
    nb = t // MOBA_BLOCK
    nblk = 8 if nb % 8 == 0 else nb
    return pl.pallas_call(
        functools.partial(_kmean_kernel, nblk=nblk),
        grid=(b, nb // nblk),
        in_specs=[pl.BlockSpec((1, nblk * MOBA_BLOCK, d_m), lambda bi, i: (bi, i, 0))],
        out_specs=pl.BlockSpec((1, nblk, d_m), lambda bi, i: (bi, i, 0)),
        out_shape=jax.ShapeDtypeStruct((b, nb, d_m), F32),
        compiler_params=_cparams("parallel", "parallel"),
        name="block_means",
    )(k_rows)


def _top_blocks(gate, valid, n_sel):
    ax = gate.ndim - 1
    nb = gate.shape[ax]
    blk = lax.broadcasted_iota(jnp.int32, gate.shape, ax)
    avail = valid
    sel = jnp.zeros(gate.shape, F32)
    for _ in range(n_sel):
        g = jnp.where(avail, gate, -jnp.inf)
        m = jnp.max(g, axis=ax, keepdims=True)
        first = jnp.min(jnp.where((g == m) & avail, blk, nb), axis=ax, keepdims=True)
        pick = blk == first
        sel = jnp.where(pick, 1.0, sel)
        avail = avail & jnp.logical_not(pick)
    return sel


def _top_blocks_t(gate, valid, n_sel):
    ax = gate.ndim - 2
    nb = gate.shape[ax]
    blk = lax.broadcasted_iota(jnp.int32, gate.shape, ax)
    avail = jnp.broadcast_to(valid, gate.shape)
    sel = jnp.zeros(gate.shape, F32)
    for _ in range(n_sel):
        g = jnp.where(avail, gate, -jnp.inf)
        m = jnp.max(g, axis=ax, keepdims=True)
        first = jnp.min(jnp.where((g == m) & avail, blk, nb), axis=ax, keepdims=True)
        pick = blk == first
        sel = jnp.where(pick, 1.0, sel)
        avail = avail & jnp.logical_not(pick)
    return sel


MOBA_HEADS_PER_STEP = 4
SUBLANES = 8
MOBA_BLOCKS_PER_ITER = SUBLANES // 2
VT_ROWS = HEAD_DIM + 2 * SUBLANES


def _moba_prompt_kernel(q_ref, k_ref, vt_ref, km_ref, o_ref, sel_scr, *, n_sel):
    i = pl.program_id(2)
    hb = q_ref.shape[1]
    nb = km_ref.shape[2]
    q32 = q_ref[0]
    q = (q32 * (HEAD_DIM ** -0.5 * math.log2(math.e))).astype(BF16)

    def scores_t(j0, n_blocks):
        rows = pl.ds(pl.multiple_of(j0 * MOBA_BLOCK, MOBA_BLOCK), n_blocks * MOBA_BLOCK)
        return _bdot(k_ref[0, :, rows, :], q, _NT)

    def weighted_values_t(j0, p):
        out = None
        for jb in range(p.shape[1] // MOBA_BLOCK):
            pv = _bdot(vt_ref[0, :, j0 + jb], p[:, jb * MOBA_BLOCK:(jb + 1) * MOBA_BLOCK], _NN)
            out = pv if out is None else out + pv
        return out

    key_pos = lax.broadcasted_iota(jnp.int32, (MOBA_BLOCK, MOBA_BLOCK), 0)
    q_pos = lax.broadcasted_iota(jnp.int32, (MOBA_BLOCK, MOBA_BLOCK), 1)
    s = jnp.where(key_pos <= q_pos, scores_t(i, 1), NEG_INF)
    m = jnp.max(s, axis=1, keepdims=True)
    p = jnp.exp2(s - m)
    acc = weighted_values_t(i, p)

    if n_sel > 0:
        gate_t = _bdot(km_ref[0], q32, _NT, precision=HI)
        blk = lax.broadcasted_iota(jnp.int32, (nb, MOBA_BLOCK), 0)
        sel_scr[...] = _top_blocks_t(gate_t, blk < i, n_sel)
        kb = MOBA_BLOCKS_PER_ITER

        def body(jj, carry):
            m, acc = carry
            j0 = jj * kb
            sel8 = sel_scr[:, pl.ds(pl.multiple_of((jj // 2) * 8, 8), 8), :]
            chosen = jnp.where(jj % 2 == 0, sel8[:, :kb], sel8[:, kb:]) > 0.0
            s = scores_t(j0, kb).reshape(hb, kb, MOBA_BLOCK, MOBA_BLOCK)
            s = jnp.where(chosen[:, :, None, :], s, NEG_INF).reshape(hb, kb * MOBA_BLOCK, MOBA_BLOCK)
            m_new = jnp.maximum(m, jnp.max(s, axis=1, keepdims=True))
            alpha = jnp.exp2(m - m_new)
            p = jnp.exp2(s - m_new)
            acc = alpha * acc + weighted_values_t(j0, p)
            return m_new, acc

        m, acc = lax.fori_loop(0, (i + kb - 1) // kb, body, (m, acc))

    o_t = acc[:, :HEAD_DIM] / acc[:, HEAD_DIM:HEAD_DIM + 1]
    for h in range(hb):
        o_ref[0, h] = o_t[h].T


def moba_prompt(q, k, vt, kmean):
    b, h, t, d = q.shape
    assert t % MOBA_BLOCK == 0
    nb = t // MOBA_BLOCK
    n_sel = min(MOBA_TOPK, nb - 1)
    nb_pad = -(-nb // 8) * 8
    kmean = jnp.pad(kmean, ((0, 0), (0, 0), (0, nb_pad - nb), (0, 0)))
    hb = MOBA_HEADS_PER_STEP
    assert h % hb == 0 and nb % MOBA_BLOCKS_PER_ITER == 0
    q_spec = pl.BlockSpec((1, hb, MOBA_BLOCK, d), lambda bi, hi, i: (bi, hi, i, 0))
    return pl.pallas_call(
        functools.partial(_moba_prompt_kernel, n_sel=n_sel),
        grid=(b, h // hb, nb),
        in_specs=[
            q_spec,
            pl.BlockSpec((1, hb, t, d), lambda bi, hi, i: (bi, hi, 0, 0)),
            pl.BlockSpec((1, hb, nb, VT_ROWS, MOBA_BLOCK), lambda bi, hi, i: (bi, hi, 0, 0, 0)),
            pl.BlockSpec((1, hb, nb_pad, d), lambda bi, hi, i: (bi, hi, 0, 0)),
        ],
        out_specs=q_spec,
        out_shape=jax.ShapeDtypeStruct((b, h, t, d), F32),
        scratch_shapes=[pltpu.VMEM((hb, nb_pad, MOBA_BLOCK), F32)],
        compiler_params=_cparams("parallel", "parallel", "arbitrary"),
        name="moba_prompt",
    )(q, k, vt, kmean)


PAGES_PER_STEP = 16
SAMPLE_BLOCKS_PER_STEP = 8


def _sample_kmean_kernel(pt_ref, *refs, ppb):
    pages, o_ref = refs[:-1], refs[-1]
    nblk = len(pages) // ppb
    n_heads, _, page_size = pages[0].shape[2:]
    ones = jnp.ones((n_heads, SUBLANES, page_size), BF16)
    for j in range(nblk):
        s = pages[j * ppb][0, 0]
        for q in range(1, ppb):
            s = s + pages[j * ppb + q][0, 0]
        hi, lo = _split(s)
        tot = _bdot(ones, hi, _NT) + _bdot(ones, lo, _NT)
        o_ref[0, j] = tot[:, 0, :] / (ppb * page_size)


def sample_block_means(cache_kt, layer, page_table):
    _, _, n_heads, d, page_size = cache_kt.shape
    db, n_pages = page_table.shape
    ppb = MOBA_BLOCK // page_size
    n_full = (n_pages * page_size) // MOBA_BLOCK
    pps = PAGES_PER_STEP
    assert n_pages % pps == 0 and pps % ppb == 0

    def page_spec(q):
        return pl.BlockSpec((1, 1, n_heads, d, page_size), lambda bi, g, pt: (layer, pt[bi, g * pps + q], 0, 0, 0))

    return pl.pallas_call(
        functools.partial(_sample_kmean_kernel, ppb=ppb),
        grid_spec=pltpu.PrefetchScalarGridSpec(
            num_scalar_prefetch=1,
            grid=(db, n_pages // pps),
            in_specs=[page_spec(q) for q in range(pps)],
            out_specs=pl.BlockSpec((1, pps // ppb, n_heads, d), lambda bi, g, pt: (bi, g, 0, 0)),
        ),
        out_shape=jax.ShapeDtypeStruct((db, n_full, n_heads, d), F32),
        compiler_params=_cparams("parallel", "arbitrary"),
        name="sample_block_means",
    )(page_table, *([cache_kt] * pps))


def _moba_sample_kernel(pt_ref, q_ref, kn_ref, vn_ref, km_ref, *refs, ppb, n_heads, n_sel):
    n_pages = (len(refs) - 5) // 2
    k_pages = refs[:n_pages]
    v_pages = refs[n_pages:2 * n_pages]
    o_ref = refs[2 * n_pages]
    sel_scr, m_scr, l_scr, acc_scr = refs[2 * n_pages + 1:]
    n = pl.program_id(1)
    q32 = q_ref[0]
    q = (q32 * HEAD_DIM ** -0.5).astype(BF16)
    page_size = k_pages[0].shape[4]

    @pl.when(n == 0)
    def _():
        if n_sel > 0:
            gate = _bdot(q32, km_ref[0], _NT, precision=HI)
            sel_scr[...] = _top_blocks(gate, jnp.ones(gate.shape, jnp.bool_), n_sel)
        s = _bdot(q, kn_ref[0], _NT)
        q_t = lax.broadcasted_iota(jnp.int32, s.shape, 1)
        k_t = lax.broadcasted_iota(jnp.int32, s.shape, 2)
        s = jnp.where(k_t <= q_t, s, NEG_INF)
        m = jnp.max(s, axis=2, keepdims=True)
        p = jnp.exp(s - m)
        m_scr[...] = m
        l_scr[...] = jnp.sum(p, axis=2, keepdims=True)
        acc_scr[...] = _bdot(p, vn_ref[0], _NN)

    if n_sel > 0:
        sel = sel_scr[...]
        blk = lax.broadcasted_iota(jnp.int32, sel.shape, 2)
        scores = []
        for pg in range(n_pages):
            b = n * (n_pages // ppb) + pg // ppb
            chosen = jnp.sum(jnp.where(blk == b, sel, 0.0), axis=2, keepdims=True) > 0.0
            scores.append(jnp.where(chosen, _bdot(q, k_pages[pg][0, 0], _NN), NEG_INF))
        s = jnp.concatenate(scores, axis=2)
        m = m_scr[...]
        m_new = jnp.maximum(m, jnp.max(s, axis=2, keepdims=True))
        alpha = jnp.exp(m - m_new)
        p = jnp.exp(s - m_new)
        m_scr[...] = m_new
        l_scr[...] = alpha * l_scr[...] + jnp.sum(p, axis=2, keepdims=True)
        acc = alpha * acc_scr[...]
        for pg in range(n_pages):
            acc = acc + _bdot(p[:, :, pg * page_size:(pg + 1) * page_size], v_pages[pg][0, 0], _NT)
        acc_scr[...] = acc

    @pl.when(n == pl.num_programs(1) - 1)
    def _():
        o_ref[0] = acc_scr[...] / l_scr[...]


def moba_sample(cache_kt, cache_vt, layer, page_table, q_rows, k_rows, v_rows, kmean):
    _, _, n_heads, d, page_size = cache_kt.shape
    db, t, _ = q_rows.shape
    n_pages = page_table.shape[1]
    ppb = MOBA_BLOCK // page_size
    n_full = (n_pages * page_size) // MOBA_BLOCK
    assert n_full * ppb == n_pages, "past rows inside the current block are not supported"
    n_sel = min(MOBA_TOPK, n_full)
    tp = -(-t // SUBLANES) * SUBLANES
    hm = lambda x: jnp.pad(_to_heads(x), ((0, 0), (0, 0), (0, tp - t), (0, 0)))
    new_spec = pl.BlockSpec((1, n_heads, tp, d), lambda bi, n, pt: (bi, 0, 0, 0))

    pps = SAMPLE_BLOCKS_PER_STEP * ppb
    assert n_pages % pps == 0

    def page_spec(q):
        return pl.BlockSpec((1, 1, n_heads, d, page_size), lambda bi, n, pt: (layer, pt[bi, n * pps + q], 0, 0, 0))

    o = pl.pallas_call(
        functools.partial(_moba_sample_kernel, ppb=ppb, n_heads=n_heads, n_sel=n_sel),
        grid_spec=pltpu.PrefetchScalarGridSpec(
            num_scalar_prefetch=1,
            grid=(db, n_pages // pps),
            in_specs=[new_spec, new_spec, new_spec,
                      pl.BlockSpec((1, n_heads, n_full, d), lambda bi, n, pt: (bi, 0, 0, 0))]
            + [page_spec(q) for q in range(pps)] * 2,
            out_specs=new_spec,
            scratch_shapes=[
                pltpu.VMEM((n_heads, tp, n_full), F32),
                pltpu.VMEM((n_heads, tp, 1), F32),
                pltpu.VMEM((n_heads, tp, 1), F32),
                pltpu.VMEM((n_heads, tp, d), F32),
            ],
        ),
        out_shape=jax.ShapeDtypeStruct((db, n_heads, tp, d), F32),
        compiler_params=_cparams("parallel", "arbitrary"),
        name="moba_sample",
    )(page_table, hm(q_rows), hm(k_rows), hm(v_rows), kmean, *([cache_kt] * pps), *([cache_vt] * pps))
    return o[:, :, :t]


def _mix_out_kernel(y_ref, r_ref, k_ref, v_ref, g_ref, om_ref, x_ref, wo_ref, lnw_ref, lnb_ref, rk_ref, o_ref,
                    *, h_rwkv, h_moba):
    acc = x_ref[0]
    for h in range(h_rwkv):
        y = y_ref[0, h]
        mean = jnp.mean(y, axis=-1, keepdims=True)
        yc = y - mean
        var = jnp.mean(yc * yc, axis=-1, keepdims=True)
        yn = yc * lax.rsqrt(var + GN_EPS) * lnw_ref[h:h + 1, :] + lnb_ref[h:h + 1, :]
        bonus = jnp.sum(r_ref[0, h] * k_ref[0, h] * rk_ref[h:h + 1, :], axis=-1, keepdims=True) * v_ref[0, h]
        o = ((yn + bonus) * g_ref[0, h]).astype(BF16)
        acc = acc + jnp.dot(o, wo_ref[h * HEAD_DIM:(h + 1) * HEAD_DIM, :], preferred_element_type=F32)
    for h in range(h_moba):
        row0 = (h_rwkv + h) * HEAD_DIM
        acc = acc + jnp.dot(om_ref[0, h].astype(BF16), wo_ref[row0:row0 + HEAD_DIM, :], preferred_element_type=F32)
    o_ref[0] = acc


def mix_out(y, r, k, v, g, o_moba, x, w_out, ln_w, ln_b, r_k, *, tt_target=512):
    b, h_rwkv, t, n = y.shape
    h_moba = o_moba.shape[1]
    d = x.shape[2]
    tt = _row_tile(t, tt_target)
    hm_r = pl.BlockSpec((1, h_rwkv, tt, n), lambda bi, i: (bi, 0, i, 0))
    hm_m = pl.BlockSpec((1, h_moba, tt, n), lambda bi, i: (bi, 0, i, 0))
    full = lambda shape: pl.BlockSpec(shape, lambda bi, i: (0,) * len(shape))
    params = [w_out, ln_w.reshape(h_rwkv, n), ln_b.reshape(h_rwkv, n), r_k.reshape(h_rwkv, n)]
    return pl.pallas_call(
        functools.partial(_mix_out_kernel, h_rwkv=h_rwkv, h_moba=h_moba),
        grid=(b, t // tt),
        in_specs=[hm_r] * 5 + [hm_m, pl.BlockSpec((1, tt, d), lambda bi, i: (bi, i, 0))] + [full(p.shape) for p in params],
        out_specs=pl.BlockSpec((1, tt, d), lambda bi, i: (bi, i, 0)),
        out_shape=jax.ShapeDtypeStruct((b, t, d), F32),
        compiler_params=_cparams("parallel", "parallel"),
        name="mix_out",
    )(y, r, k, v, g, o_moba, x, *params)


def _to_heads(x_rows):
    b, t, dm = x_rows.shape
    return x_rows.reshape(b, t, dm // HEAD_DIM, HEAD_DIM).transpose(0, 2, 1, 3)


def _mixer_front(y, lw, p_prev_rows, wkv_prev, cos, sin, *, head_major):
    b, t, d = y.shape
    flat = y.reshape(b * t, d)
    pr = norm_proj(flat, lw['mix_norm'], lw['w_in_r']).reshape(b, t, -1)
    pm = norm_proj(flat, lw['mix_norm'], lw['w_in_m']).reshape(b, t, -1)
    shift = rms_norm_rows(y[:, -1], lw['mix_norm'])
    r, log_w, k, v, kk, kka, g, *cum = rwkv_prep(pr, p_prev_rows, lw)
    if cum:
        y_wkv, wkv_new = wkv_chunk_scan(r, log_w, cum[0], k, v, kk, kka, wkv_prev)
    else:
        y_wkv, wkv_new = wkv_scan(r, log_w, k, v, kk, kka, wkv_prev)
    k_rows, *moba_in = rope_qkv(pm, cos, sin, head_major=head_major)
    d_m = k_rows.shape[2]
    v_rows = pm[:, :, 2 * d_m:]
    return dict(r=r, k=k, v=v, g=g, y=y_wkv, wkv=wkv_new, shift=shift, k_rows=k_rows, v_rows=v_rows, moba_in=moba_in)


@jax.jit
def kernel(x_prompt, x_sample, cache_k, cache_v, state_wkv, state_shift, page_table, ffn1_norm, ffn1_w_gate,
           ffn1_w_up, ffn1_w_down, mix_norm, w_in, w_out, rwkv_mu, rwkv_w0, rwkv_w2, rwkv_a0, rwkv_a2, rwkv_g2,
           rwkv_k_k, rwkv_k_a, rwkv_r_k, rwkv_ln_w, rwkv_ln_b, ffn2_norm, ffn2_w_gate, ffn2_w_up, ffn2_w_down,
           final_norm):
    b, t, d = x_prompt.shape
    db, ts, _ = x_sample.shape
    depth = w_in.shape[0]
    d_r = rwkv_w0.shape[1]
    h_rwkv = d_r // HEAD_DIM
    c_r = rwkv_mu.shape[1]
    d_m = (w_in.shape[2] - c_r) // 3
    h_moba = d_m // HEAD_DIM
    past_len = page_table.shape[1] * cache_k.shape[2]
    ck = cache_k.transpose(0, 1, 3, 4, 2)
    cv = cache_v.transpose(0, 1, 3, 4, 2)
    cos_p, sin_p = rope_tables(jnp.arange(t), h_moba)
    cos_s, sin_s = rope_tables(past_len + jnp.arange(ts), h_moba)

    yp, ys = x_prompt, x_sample
    outs = {n: [] for n in ('kp', 'vp', 'ks', 'vs', 'wp', 'ws', 'sp', 'ss')}
    for l in range(depth):
        lw = {'mix_norm': mix_norm[l], 'w_in_r': w_in[l, :, :c_r].astype(BF16), 'w_in_m': w_in[l, :, c_r:].astype(BF16),
              'mu': rwkv_mu[l], 'w0': rwkv_w0[l], 'w2': rwkv_w2[l], 'a0': rwkv_a0[l], 'a2': rwkv_a2[l],
              'g2': rwkv_g2[l], 'k_k': rwkv_k_k[l], 'k_a': rwkv_k_a[l]}
        wo = w_out[l].astype(BF16)
        f1 = (ffn1_norm[l], ffn1_w_gate[l].astype(BF16), ffn1_w_up[l].astype(BF16), ffn1_w_down[l].astype(BF16))
        f2 = (ffn2_norm[l], ffn2_w_gate[l].astype(BF16), ffn2_w_up[l].astype(BF16), ffn2_w_down[l].astype(BF16))

        yp = ffn_half(yp.reshape(b * t, d), *f1).reshape(b, t, d)
        ys = ffn_half(ys.reshape(db * ts, d), *f1).reshape(db, ts, d)

        fp = _mixer_front(yp, lw, jnp.zeros((b, c_r), F32), jnp.zeros((b, h_rwkv, HEAD_DIM, HEAD_DIM), F32),
                          cos_p, sin_p, head_major=True)
        p_prev_s = norm_proj(state_shift[l], lw['mix_norm'], lw['w_in_r'], normalize=False)
        fs = _mixer_front(ys, lw, p_prev_s, state_wkv[l], cos_s, sin_s, head_major=False)

        kmean_p = _to_heads(block_means(fp['k_rows']))
        om_p = moba_prompt(*fp['moba_in'], kmean_p)
        kmean_s = sample_block_means(ck, l, page_table).transpose(0, 2, 1, 3)
        om_s = moba_sample(ck, cv, l, page_table, fs['moba_in'][0], fs['k_rows'], fs['v_rows'], kmean_s)

        mix = lambda f, om, y: mix_out(f['y'], f['r'], f['k'], f['v'], f['g'], om, y, wo,
                                       rwkv_ln_w[l], rwkv_ln_b[l], rwkv_r_k[l])
        yp = mix(fp, om_p, yp)
        ys = mix(fs, om_s, ys)

        yp = ffn_half(yp.reshape(b * t, d), *f2).reshape(b, t, d)
        ys = ffn_half(ys.reshape(db * ts, d), *f2).reshape(db, ts, d)

        outs['kp'].append(fp['k_rows'].reshape(b, t, h_moba, HEAD_DIM))
        outs['vp'].append(fp['v_rows'].reshape(b, t, h_moba, HEAD_DIM))
        outs['ks'].append(fs['k_rows'].reshape(db, ts, h_moba, HEAD_DIM))
        outs['vs'].append(fs['v_rows'].reshape(db, ts, h_moba, HEAD_DIM))
        outs['wp'].append(fp['wkv'])
        outs['ws'].append(fs['wkv'])
        outs['sp'].append(fp['shift'])
        outs['ss'].append(fs['shift'])

    y_prompt = rms_norm_rows(yp.reshape(b * t, d), final_norm).reshape(b, t, d)
    y_sample = rms_norm_rows(ys.reshape(db * ts, d), final_norm).reshape(db, ts, d)
    st = lambda n: jnp.stack(outs[n])
    return (y_prompt, y_sample, st('kp'), st('vp'), st('ks'), st('vs'), st('wp'), st('ws'), st('sp'), st('ss'))
```

```python
import functools
import math

import jax
import jax.numpy as jnp
from jax import lax
from jax.experimental import pallas as pl
from jax.experimental.pallas import tpu as pltpu

F32 = jnp.float32
BF16 = jnp.bfloat16
HI = lax.Precision.HIGHEST

HEAD_DIM = 64
MOBA_BLOCK = 256
MOBA_TOPK = 3
ROPE_THETA = 10000.0
NORM_EPS = 1e-6
GN_EPS = 64e-5
NEG_INF = -1e30
DECAY_LORA = 64
ICLR_LORA = 64
GATE_LORA = 128
WKV_CHUNKS_PER_ITER = 4
LANES = 128
WKV_CHUNK = 64

VMEM_LIMIT = 56 * 1024 * 1024


def _cparams(*sem):
    return pltpu.CompilerParams(dimension_semantics=sem, vmem_limit_bytes=VMEM_LIMIT)


def _row_tile(n, target):
    t = min(n, target)
    while n % t:
        t //= 2
    return t


def _nt_dot(a, b, precision=None):
    return lax.dot_general(a, b, (((1,), (1,)), ((), ())), precision=precision, preferred_element_type=F32)


def _ffn_kernel(x_ref, g_ref, wg_ref, wu_ref, wd_ref, o_ref, h_scr, acc_scr):
    j = pl.program_id(1)

    @pl.when(j == 0)
    def _():
        x = x_ref[...]
        ms = jnp.mean(x * x, axis=-1, keepdims=True)
        h_scr[...] = (x * lax.rsqrt(ms + NORM_EPS) * g_ref[...]).astype(BF16)
        acc_scr[...] = jnp.zeros_like(acc_scr)

    h = h_scr[...]
    a = jnp.dot(h, wg_ref[...], preferred_element_type=F32)
    b = jnp.dot(h, wu_ref[...], preferred_element_type=F32)
    u = a * jax.nn.sigmoid(a) * b
    acc_scr[...] += jnp.dot(u.astype(BF16), wd_ref[...], preferred_element_type=F32)

    @pl.when(j == pl.num_programs(1) - 1)
    def _():
        o_ref[...] = x_ref[...] + 0.5 * acc_scr[...]


def ffn_half(x, g, wg, wu, wd, *, tm_target=512, tf=1408):
    n, d = x.shape
    f = wg.shape[1]
    tm = _row_tile(n, tm_target)
    assert f % tf == 0
    return pl.pallas_call(
        _ffn_kernel,
        grid=(n // tm, f // tf),
        in_specs=[
            pl.BlockSpec((tm, d), lambda i, j: (i, 0)),
            pl.BlockSpec((1, d), lambda i, j: (0, 0)),
            pl.BlockSpec((d, tf), lambda i, j: (0, j)),
            pl.BlockSpec((d, tf), lambda i, j: (0, j)),
            pl.BlockSpec((tf, d), lambda i, j: (j, 0)),
        ],
        out_specs=pl.BlockSpec((tm, d), lambda i, j: (i, 0)),
        out_shape=jax.ShapeDtypeStruct((n, d), F32),
        scratch_shapes=[pltpu.VMEM((tm, d), BF16), pltpu.VMEM((tm, d), F32)],
        compiler_params=_cparams("parallel", "arbitrary"),
        name="ffn_half",
    )(x, g.reshape(1, d), wg, wu, wd)


def _proj_kernel(x_ref, g_ref, w_ref, o_ref, h_scr, *, normalize):
    @pl.when(pl.program_id(1) == 0)
    def _():
        x = x_ref[...]
        if normalize:
            ms = jnp.mean(x * x, axis=-1, keepdims=True)
            x = x * lax.rsqrt(ms + NORM_EPS) * g_ref[...]
        h_scr[...] = x.astype(BF16)

    o_ref[...] = jnp.dot(h_scr[...], w_ref[...], preferred_element_type=F32)


def norm_proj(x, g, w, *, normalize=True, tm_target=1024):
    n, d = x.shape
    c = w.shape[1]
    tm = _row_tile(n, tm_target)
    tn = c
    return pl.pallas_call(
        functools.partial(_proj_kernel, normalize=normalize),
        grid=(n // tm, c // tn),
        in_specs=[
            pl.BlockSpec((tm, d), lambda i, j: (i, 0)),
            pl.BlockSpec((1, d), lambda i, j: (0, 0)),
            pl.BlockSpec((d, tn), lambda i, j: (0, j)),
        ],
        out_specs=pl.BlockSpec((tm, tn), lambda i, j: (i, j)),
        out_shape=jax.ShapeDtypeStruct((n, c), F32),
        scratch_shapes=[pltpu.VMEM((tm, d), BF16)],
        compiler_params=_cparams("parallel", "arbitrary"),
        name="norm_proj",
    )(x, g.reshape(1, d), w)


def _rms_kernel(x_ref, g_ref, o_ref):
    x = x_ref[...]
    ms = jnp.mean(x * x, axis=-1, keepdims=True)
    o_ref[...] = x * lax.rsqrt(ms + NORM_EPS) * g_ref[...]


def rms_norm_rows(x, g, *, tm_target=1024):
    n, d = x.shape
    tm = _row_tile(n, tm_target)
    return pl.pallas_call(
        _rms_kernel,
        grid=(n // tm,),
        in_specs=[pl.BlockSpec((tm, d), lambda i: (i, 0)), pl.BlockSpec((1, d), lambda i: (0, 0))],
        out_specs=pl.BlockSpec((tm, d), lambda i: (i, 0)),
        out_shape=jax.ShapeDtypeStruct((n, d), F32),
        compiler_params=_cparams("parallel"),
        name="rms_norm",
    )(x, g.reshape(1, d))


def _head_block_diag(n):
    r = lax.broadcasted_iota(jnp.int32, (n, n), 0) // HEAD_DIM
    c = lax.broadcasted_iota(jnp.int32, (n, n), 1) // HEAD_DIM
    return (r == c).astype(F32)


def _prep_kernel(*refs, has_prev_tile, with_cum, n_heads):
    if has_prev_tile:
        p_ref, prev8_ref, pprev_ref = refs[:3]
        rest = refs[3:]
    else:
        p_ref, pprev_ref = refs[:2]
        prev8_ref = None
        rest = refs[2:]
    mu_ref, w0_ref, w2_ref, a0_ref, a2_ref, g2_ref, kk_ref, ka_ref = rest[:8]
    r_out, w_out, k_out, v_out, kk_out, kka_out, g_out = rest[8:15]
    c_out = rest[15] if with_cum else None
    d_r = n_heads * HEAD_DIM
    p = p_ref[0]
    tt = p.shape[0]
    prev_row = pprev_ref[0]
    if has_prev_tile:
        prev_row = jnp.where(pl.program_id(1) == 0, prev_row, prev8_ref[0, 7:8, :])
    row = lax.broadcasted_iota(jnp.int32, p.shape, 0)
    if tt % 8 == 0:
        rolled = pltpu.roll(p, 1, 0)
    else:
        rolled = jnp.concatenate([p[tt - 1:], p[:tt - 1]], axis=0)
    p_shift = jnp.where(row == 0, prev_row, rolled)
    xm = p + (p_shift - p) * mu_ref[...]
    r = xm[:, :d_r]
    k = xm[:, d_r:2 * d_r]
    v = xm[:, 2 * d_r:3 * d_r]
    c0 = 3 * d_r
    xw = xm[:, c0:c0 + DECAY_LORA]
    xa = xm[:, c0 + DECAY_LORA:c0 + DECAY_LORA + ICLR_LORA]
    xg = xm[:, c0 + DECAY_LORA + ICLR_LORA:]
    z = -(w0_ref[...] + _dot_split(jnp.tanh(xw), w2_ref[...]))
    softplus = jnp.maximum(z, 0.0) + jnp.log(1.0 + jnp.exp(-jnp.abs(z)))
    log_decay = -jnp.exp(-softplus - 0.5)
    if with_cum:
        ci = lax.broadcasted_iota(jnp.int32, (tt, tt), 0)
        cj = lax.broadcasted_iota(jnp.int32, (tt, tt), 1)
        tri = ((ci // WKV_CHUNK == cj // WKV_CHUNK) & (cj <= ci)).astype(BF16)
        cum = sum(jnp.dot(tri, piece, preferred_element_type=F32) for piece in _split3(log_decay))
    a = jax.nn.sigmoid(a0_ref[...] + _dot_split(xa, a2_ref[...]))
    g = _dot_split(jax.nn.sigmoid(xg), g2_ref[...])
    kk = k * kk_ref[...]
    pair_diag = _head_block_diag(LANES).astype(BF16)
    sq = kk * kk
    ss = jnp.concatenate(
        [sum(jnp.dot(piece, pair_diag, preferred_element_type=F32) for piece in _split3(sq[:, c:c + LANES]))
         for c in range(0, d_r, LANES)], axis=1)
    kk = kk / jnp.maximum(jnp.sqrt(ss), 1e-12)
    k2 = k * (1.0 + (a - 1.0) * ka_ref[...])
    kka = kk * a
    for h in range(n_heads):
        sl = slice(h * HEAD_DIM, (h + 1) * HEAD_DIM)
        r_out[0, h] = r[:, sl]
        w_out[0, h] = log_decay[:, sl]
        if with_cum:
            c_out[0, h] = cum[:, sl]
        k_out[0, h] = k2[:, sl]
        v_out[0, h] = v[:, sl]
        kk_out[0, h] = kk[:, sl]
        kka_out[0, h] = kka[:, sl]
        g_out[0, h] = g[:, sl]


def rwkv_prep(pr, p_prev, lp, *, tt_target=256):
    b, t, c = pr.shape
    d_r = lp['w0'].shape[0]
    n_heads = d_r // HEAD_DIM
    tt = _row_tile(t, tt_target)
    nt = t // tt
    has_prev = nt > 1
    with_cum = tt % WKV_CHUNK == 0
    n_out = 8 if with_cum else 7
    row = lambda x: x.reshape(1, -1)
    full = lambda shape: pl.BlockSpec(shape, lambda bi, i: (0,) * len(shape))
    in_specs = [pl.BlockSpec((1, tt, c), lambda bi, i: (bi, i, 0))]
    args = [pr]
    if has_prev:
        in_specs.append(pl.BlockSpec((1, 8, c), lambda bi, i: (bi, jnp.maximum(i * (tt // 8) - 1, 0), 0)))
        args.append(pr)
    in_specs.append(pl.BlockSpec((1, 1, c), lambda bi, i: (bi, 0, 0)))
    args.append(p_prev.reshape(b, 1, c))
    params = [row(lp['mu']), row(lp['w0']), lp['w2'], row(lp['a0']), lp['a2'], lp['g2'], row(lp['k_k']), row(lp['k_a'])]
    in_specs += [full(x.shape) for x in params]
    args += params
    hm = jax.ShapeDtypeStruct((b, n_heads, t, HEAD_DIM), F32)
    hm_spec = pl.BlockSpec((1, n_heads, tt, HEAD_DIM), lambda bi, i: (bi, 0, i, 0))
    return pl.pallas_call(
        functools.partial(_prep_kernel, has_prev_tile=has_prev, with_cum=with_cum, n_heads=n_heads),
        grid=(b, nt),
        in_specs=in_specs,
        out_specs=[hm_spec] * n_out,
        out_shape=[hm] * n_out,
        compiler_params=_cparams("parallel", "arbitrary"),
        name="rwkv_prep",
    )(*args)


def _scan_kernel(r_ref, lw_ref, k_ref, v_ref, kk_ref, kka_ref, s0_ref, y_ref, s_out_ref, s_scr, *, n_heads, tc):
    @pl.when(pl.program_id(1) == 0)
    def _():
        s_scr[...] = s0_ref[0]

    eye = (lax.broadcasted_iota(jnp.int32, (HEAD_DIM, HEAD_DIM), 0)
           == lax.broadcasted_iota(jnp.int32, (HEAD_DIM, HEAD_DIM), 1))

    def body(t, carry):
        for h in range(n_heads):
            s = s_scr[h]
            row = lambda ref: ref[0, h, pl.ds(t, 1), :]
            sa = -jnp.sum(s * row(kk_ref), axis=1, keepdims=True)
            v_col = jnp.sum(jnp.where(eye, row(v_ref), 0.0), axis=1, keepdims=True)
            s = s * jnp.exp(row(lw_ref)) + sa * row(kka_ref) + v_col * row(k_ref)
            s_scr[h] = s
            y_col = jnp.sum(s * row(r_ref), axis=1, keepdims=True)
            y_ref[0, h, pl.ds(t, 1), :] = jnp.sum(jnp.where(eye, y_col, 0.0), axis=0, keepdims=True)
        return carry

    lax.fori_loop(0, tc, body, 0)
    s_out_ref[0] = s_scr[...]


def _bdot(a, b, contract, precision=None):
    if precision is None:
        a, b = a.astype(BF16), b.astype(BF16)
    return lax.dot_general(a, b, (((contract[0],), (contract[1],)), ((0,), (0,))), precision=precision,
                           preferred_element_type=F32)


_NN, _NT, _TN = (2, 1), (2, 2), (1, 1)


def _split(x):
    hi = x.astype(BF16)
    return hi, (x - hi.astype(F32)).astype(BF16)


def _split3(x):
    hi = x.astype(BF16)
    rest = x - hi.astype(F32)
    mid = rest.astype(BF16)
    return hi, mid, (rest - mid.astype(F32)).astype(BF16)


def _dot_split(a, b):
    (ah, al), (bh, bl) = _split(a), _split(b)
    dot = lambda x, y: jnp.dot(x, y, preferred_element_type=F32)
    return dot(ah, bh) + (dot(ah, bl) + dot(al, bh))


def _bdot_split(a, b, contract):
    (ah, al), (bh, bl) = a, b
    return _bdot(ah, bh, contract) + (_bdot(ah, bl, contract) + _bdot(al, bh, contract))


def _unit_lower_inverse(t_strict):
    n = t_strict.shape[1]
    ri = lax.broadcasted_iota(jnp.int32, (n, n), 0)
    ci = lax.broadcasted_iota(jnp.int32, (n, n), 1)
    same = lambda size: (ri // size) == (ci // size)
    mm = lambda x, y: _bdot_split(x, y, _NN)
    neg32 = jnp.where(same(8), -t_strict, 0.0)
    neg = _split(neg32)
    n2 = _split(mm(neg, neg))
    n4 = _split(mm(n2, n2))
    inv = (ri == ci).astype(F32) + neg32
    inv = inv + mm(_split(inv), n2)
    inv = inv + mm(_split(inv), n4)
    for size in (16, 32, 64):
        off = _split(jnp.where(same(size) & jnp.logical_not(same(size // 2)), t_strict, 0.0))
        inv_s = _split(inv)
        inv = inv - mm(inv_s, _split(mm(off, inv_s)))
    return inv


def _chunk_scan_kernel(r_ref, lw_ref, c_ref, k_ref, v_ref, kk_ref, kka_ref, s0_ref, y_ref, s_out_ref, s_scr,
                       *, n_heads, n_chunks):
    L = WKV_CHUNK

    @pl.when(pl.program_id(1) == 0)
    def _():
        s_scr[...] = s0_ref[0]

    ri = lax.broadcasted_iota(jnp.int32, (2 * L, 2 * L), 0)
    ci = lax.broadcasted_iota(jnp.int32, (2 * L, 2 * L), 1)
    keep = ((ri < L) & ((ci % L) < ri)) | ((ri >= L) & ((ci % L) <= (ri - L)))

    G = WKV_CHUNKS_PER_ITER

    def body(cidx, carry):
        rows = pl.ds(pl.multiple_of(cidx * (G * L), G * L), G * L)
        ld = lambda ref: ref[0, :, rows, :].reshape(n_heads * G, L, HEAD_DIM)
        r, lw, c, k, v, a, b = (ld(x) for x in (r_ref, lw_ref, c_ref, k_ref, v_ref, kk_ref, kka_ref))
        c_last = c[:, L - 1:L, :]
        inv_g = jnp.exp(-c)
        a_t = a * jnp.exp(c - lw)
        b_t = b * inv_g
        k_t = k * inv_g
        r_t = r * jnp.exp(c)
        to_end = jnp.exp(c_last - c)
        b_end = b * to_end
        k_end = k * to_end
        coef = _bdot_split(_split(jnp.concatenate([a_t, r_t], axis=1)),
                           _split(jnp.concatenate([b_t, k_t], axis=1)), _NT)
        coef = jnp.where(keep, coef, 0.0)
        t_ab, t_ak = coef[:, :L, :L], coef[:, :L, L:]
        a_rb, a_rk = coef[:, L:, :L], coef[:, L:, L:]
        m_inv = _unit_lower_inverse(t_ab)
        w_mat = _bdot(m_inv, a_t, _NN)
        u_mat = -_bdot(m_inv, _bdot(t_ak, v, _NN), _NN)
        y_intra = _bdot(a_rb, u_mat, _NN) + _bdot(a_rk, v, _NN)
        r_w = r_t - _bdot(a_rb, w_mat, _NN)
        h_mat = _bdot(u_mat, b_end, _TN) + _bdot(v, k_end, _TN)
        per_chunk = lambda x: x.reshape(n_heads, G, *x.shape[1:])
        r_w, y_intra, w_mat, b_end, h_mat, decay_end = (
            per_chunk(x) for x in (r_w, y_intra, w_mat, b_end, h_mat, jnp.exp(c_last)))
        s0 = s_scr[...]
        ys = []
        for g in range(G):
            ys.append(_bdot(r_w[:, g], s0, _NT) + y_intra[:, g])
            p = _bdot(s0, w_mat[:, g], _NT)
            s0 = s0 * decay_end[:, g] - _bdot(p, b_end[:, g], _NN) + h_mat[:, g]
        y_ref[0, :, rows, :] = jnp.concatenate(ys, axis=1)
        s_scr[...] = s0
        return carry

    lax.fori_loop(0, n_chunks // G, body, 0)
    s_out_ref[0] = s_scr[...]


def wkv_chunk_scan(r, lw, c, k, v, kk, kka, s0, *, tc_target=256):
    b, h, t, n = r.shape
    tc = _row_tile(t, tc_target)
    assert tc % (WKV_CHUNK * WKV_CHUNKS_PER_ITER) == 0 and n == WKV_CHUNK
    seq = pl.BlockSpec((1, h, tc, n), lambda bi, i: (bi, 0, i, 0))
    st = pl.BlockSpec((1, h, n, n), lambda bi, i: (bi, 0, 0, 0))
    return pl.pallas_call(
        functools.partial(_chunk_scan_kernel, n_heads=h, n_chunks=tc // WKV_CHUNK),
        grid=(b, t // tc),
        in_specs=[seq] * 7 + [st],
        out_specs=[seq, st],
        out_shape=[jax.ShapeDtypeStruct((b, h, t, n), F32), jax.ShapeDtypeStruct((b, h, n, n), F32)],
        scratch_shapes=[pltpu.VMEM((h, n, n), F32)],
        compiler_params=_cparams("parallel", "arbitrary"),
        name="wkv_chunk_scan",
    )(r, lw, c, k, v, kk, kka, s0)


def wkv_scan(r, w, k, v, kk, kka, s0, *, tc_target=256):
    b, h, t, n = r.shape
    tc = _row_tile(t, tc_target)
    seq = pl.BlockSpec((1, h, tc, n), lambda bi, c: (bi, 0, c, 0))
    st = pl.BlockSpec((1, h, n, n), lambda bi, c: (bi, 0, 0, 0))
    return pl.pallas_call(
        functools.partial(_scan_kernel, n_heads=h, tc=tc),
        grid=(b, t // tc),
        in_specs=[seq] * 6 + [st],
        out_specs=[seq, st],
        out_shape=[jax.ShapeDtypeStruct((b, h, t, n), F32), jax.ShapeDtypeStruct((b, h, n, n), F32)],
        scratch_shapes=[pltpu.VMEM((h, n, n), F32)],
        compiler_params=_cparams("parallel", "arbitrary"),
        name="wkv_scan",
    )(r, w, k, v, kk, kka, s0)


def _rope_kernel(p_ref, cos_ref, sin_ref, krow_ref, q_out, k_out=None, vt_out=None, *, n_heads, head_major):
    d_m = n_heads * HEAD_DIM
    half = HEAD_DIM // 2
    cos = cos_ref[...]
    sin = sin_ref[...]
    first_half = (lax.broadcasted_iota(jnp.int32, cos.shape, 1) % HEAD_DIM) < half

    def rot(x):
        if x.shape[0] % 8 == 0:
            fwd = pltpu.roll(x, d_m - half, 1)
            bwd = pltpu.roll(x, half, 1)
        else:
            fwd = jnp.concatenate([x[:, half:], x[:, :half]], axis=1)
            bwd = jnp.concatenate([x[:, d_m - half:], x[:, :d_m - half]], axis=1)
        return x * cos + jnp.where(first_half, fwd, bwd) * sin

    p = p_ref[0]
    q = rot(p[:, :d_m])
    k = rot(p[:, d_m:2 * d_m])
    krow_ref[0] = k
    if not head_major:
        q_out[0] = q
        return
    v = p[:, 2 * d_m:]
    for h in range(n_heads):
        sl = slice(h * HEAD_DIM, (h + 1) * HEAD_DIM)
        q_out[0, h] = q[:, sl]
        k_out[0, h] = k[:, sl].astype(BF16)
        for jb in range(p.shape[0] // MOBA_BLOCK):
            vt = v[jb * MOBA_BLOCK:(jb + 1) * MOBA_BLOCK, sl].T
            ones = jnp.ones((VT_ROWS - HEAD_DIM, MOBA_BLOCK), F32)
            vt_out[0, h, jb] = jnp.concatenate([vt, ones], axis=0).astype(BF16)


def rope_qkv(pm, cos, sin, *, head_major, tt_target=512):
    b, t, c = pm.shape
    d_m = c // 3
    n_heads = d_m // HEAD_DIM
    tt = _row_tile(t, tt_target)
    rows_spec = pl.BlockSpec((1, tt, d_m), lambda bi, i: (bi, i, 0))
    rows = jax.ShapeDtypeStruct((b, t, d_m), F32)
    if head_major:
        assert tt % MOBA_BLOCK == 0
        hm_spec = pl.BlockSpec((1, n_heads, tt, HEAD_DIM), lambda bi, i: (bi, 0, i, 0))
        hm = lambda dt: jax.ShapeDtypeStruct((b, n_heads, t, HEAD_DIM), dt)
        nbt = tt // MOBA_BLOCK
        out_specs = [rows_spec, hm_spec, hm_spec,
                     pl.BlockSpec((1, n_heads, nbt, VT_ROWS, MOBA_BLOCK), lambda bi, i: (bi, 0, i, 0, 0))]
        out_shape = [rows, hm(F32), hm(BF16),
                     jax.ShapeDtypeStruct((b, n_heads, t // MOBA_BLOCK, VT_ROWS, MOBA_BLOCK), BF16)]
    else:
        out_specs = [rows_spec, rows_spec]
        out_shape = [rows, rows]
    return pl.pallas_call(
        functools.partial(_rope_kernel, n_heads=n_heads, head_major=head_major),
        grid=(b, t // tt),
        in_specs=[
            pl.BlockSpec((1, tt, c), lambda bi, i: (bi, i, 0)),
            pl.BlockSpec((tt, d_m), lambda bi, i: (i, 0)),
            pl.BlockSpec((tt, d_m), lambda bi, i: (i, 0)),
        ],
        out_specs=out_specs,
        out_shape=out_shape,
        compiler_params=_cparams("parallel", "parallel"),
        name="rope_qkv",
    )(pm, cos, sin)


def rope_tables(pos, n_heads):
    half = HEAD_DIM // 2
    inv_freq = ROPE_THETA ** (-jnp.arange(half, dtype=F32) / half)
    ang = pos.astype(F32)[:, None] * inv_freq[None, :]
    cos = jnp.cos(ang)
    sin = jnp.sin(ang)
    cos = jnp.tile(jnp.concatenate([cos, cos], axis=1), (1, n_heads))
    sin = jnp.tile(jnp.concatenate([-sin, sin], axis=1), (1, n_heads))
    return cos, sin


def _kmean_kernel(k_ref, o_ref, *, nblk):
    for j in range(nblk):
        o_ref[0, j:j + 1, :] = jnp.mean(k_ref[0, j * MOBA_BLOCK:(j + 1) * MOBA_BLOCK, :], axis=0, keepdims=True)


def block_means(k_rows):
    b, t, d_m = k_rows.shape
    nb = t // MOBA_BLOCK
    nblk = 8 if nb % 8 == 0 else nb
    return pl.pallas_call(
        functools.partial(_kmean_kernel, nblk=nblk),
        grid=(b, nb // nblk),
        in_specs=[pl.BlockSpec((1, nblk * MOBA_BLOCK, d_m), lambda bi, i: (bi, i, 0))],
        out_specs=pl.BlockSpec((1, nblk, d_m), lambda bi, i: (bi, i, 0)),
        out_shape=jax.ShapeDtypeStruct((b, nb, d_m), F32),
        compiler_params=_cparams("parallel", "parallel"),
        name="block_means",
    )(k_rows)


def _top_blocks(gate, valid, n_sel):
    ax = gate.ndim - 1
    nb = gate.shape[ax]
    blk = lax.broadcasted_iota(jnp.int32, gate.shape, ax)
    avail = valid
    sel = jnp.zeros(gate.shape, F32)
    for _ in range(n_sel):
        g = jnp.where(avail, gate, -jnp.inf)
        m = jnp.max(g, axis=ax, keepdims=True)
        first = jnp.min(jnp.where((g == m) & avail, blk, nb), axis=ax, keepdims=True)
        pick = blk == first
        sel = jnp.where(pick, 1.0, sel)
        avail = avail & jnp.logical_not(pick)
    return sel


def _top_blocks_t(gate, valid, n_sel):
    ax = gate.ndim - 2
    nb = gate.shape[ax]
    blk = lax.broadcasted_iota(jnp.int32, gate.shape, ax)
    avail = jnp.broadcast_to(valid, gate.shape)
    sel = jnp.zeros(gate.shape, F32)
    for _ in range(n_sel):
        g = jnp.where(avail, gate, -jnp.inf)
        m = jnp.max(g, axis=ax, keepdims=True)
        first = jnp.min(jnp.where((g == m) & avail, blk, nb), axis=ax, keepdims=True)
        pick = blk == first
        sel = jnp.where(pick, 1.0, sel)
        avail = avail & jnp.logical_not(pick)
    return sel


MOBA_HEADS_PER_STEP = 4
SUBLANES = 8
MOBA_BLOCKS_PER_ITER = SUBLANES // 2
VT_ROWS = HEAD_DIM + 2 * SUBLANES


def _moba_prompt_kernel(q_ref, k_ref, vt_ref, km_ref, o_ref, sel_scr, *, n_sel):
    i = pl.program_id(2)
    hb = q_ref.shape[1]
    nb = km_ref.shape[2]
    q32 = q_ref[0]
    q = (q32 * (HEAD_DIM ** -0.5 * math.log2(math.e))).astype(BF16)

    def scores_t(j0, n_blocks):
        rows = pl.ds(pl.multiple_of(j0 * MOBA_BLOCK, MOBA_BLOCK), n_blocks * MOBA_BLOCK)
        return _bdot(k_ref[0, :, rows, :], q, _NT)

    def weighted_values_t(j0, p):
        out = None
        for jb in range(p.shape[1] // MOBA_BLOCK):
            pv = _bdot(vt_ref[0, :, j0 + jb], p[:, jb * MOBA_BLOCK:(jb + 1) * MOBA_BLOCK], _NN)
            out = pv if out is None else out + pv
        return out

    key_pos = lax.broadcasted_iota(jnp.int32, (MOBA_BLOCK, MOBA_BLOCK), 0)
    q_pos = lax.broadcasted_iota(jnp.int32, (MOBA_BLOCK, MOBA_BLOCK), 1)
    s = jnp.where(key_pos <= q_pos, scores_t(i, 1), NEG_INF)
    m = jnp.max(s, axis=1, keepdims=True)
    p = jnp.exp2(s - m)
    acc = weighted_values_t(i, p)

    if n_sel > 0:
        gate_t = _bdot(km_ref[0], q32, _NT, precision=HI)
        blk = lax.broadcasted_iota(jnp.int32, (nb, MOBA_BLOCK), 0)
        sel_scr[...] = _top_blocks_t(gate_t, blk < i, n_sel)
        kb = MOBA_BLOCKS_PER_ITER

        def body(jj, carry):
            m, acc = carry
            j0 = jj * kb
            sel8 = sel_scr[:, pl.ds(pl.multiple_of((jj // 2) * 8, 8), 8), :]
            chosen = jnp.where(jj % 2 == 0, sel8[:, :kb], sel8[:, kb:]) > 0.0
            s = scores_t(j0, kb).reshape(hb, kb, MOBA_BLOCK, MOBA_BLOCK)
            s = jnp.where(chosen[:, :, None, :], s, NEG_INF).reshape(hb, kb * MOBA_BLOCK, MOBA_BLOCK)
            m_new = jnp.maximum(m, jnp.max(s, axis=1, keepdims=True))
            alpha = jnp.exp2(m - m_new)
            p = jnp.exp2(s - m_new)
            acc = alpha * acc + weighted_values_t(j0, p)
            return m_new, acc

        m, acc = lax.fori_loop(0, (i + kb - 1) // kb, body, (m, acc))

    o_t = acc[:, :HEAD_DIM] / acc[:, HEAD_DIM:HEAD_DIM + 1]
    for h in range(hb):
        o_ref[0, h] = o_t[h].T


def moba_prompt(q, k, vt, kmean):
    b, h, t, d = q.shape
    assert t % MOBA_BLOCK == 0
    nb = t // MOBA_BLOCK
    n_sel = min(MOBA_TOPK, nb - 1)
    nb_pad = -(-nb // 8) * 8
    kmean = jnp.pad(kmean, ((0, 0), (0, 0), (0, nb_pad - nb), (0, 0)))
    hb = MOBA_HEADS_PER_STEP
    assert h % hb == 0 and nb % MOBA_BLOCKS_PER_ITER == 0
    q_spec = pl.BlockSpec((1, hb, MOBA_BLOCK, d), lambda bi, hi, i: (bi, hi, i, 0))
    return pl.pallas_call(
        functools.partial(_moba_prompt_kernel, n_sel=n_sel),
        grid=(b, h // hb, nb),
        in_specs=[
            q_spec,
            pl.BlockSpec((1, hb, t, d), lambda bi, hi, i: (bi, hi, 0, 0)),
            pl.BlockSpec((1, hb, nb, VT_ROWS, MOBA_BLOCK), lambda bi, hi, i: (bi, hi, 0, 0, 0)),
            pl.BlockSpec((1, hb, nb_pad, d), lambda bi, hi, i: (bi, hi, 0, 0)),
        ],
        out_specs=q_spec,
        out_shape=jax.ShapeDtypeStruct((b, h, t, d), F32),
        scratch_shapes=[pltpu.VMEM((hb, nb_pad, MOBA_BLOCK), F32)],
        compiler_params=_cparams("parallel", "parallel", "arbitrary"),
        name="moba_prompt",
    )(q, k, vt, kmean)


PAGES_PER_STEP = 16
SAMPLE_BLOCKS_PER_STEP = 8


def _sample_kmean_kernel(pt_ref, *refs, ppb):
    pages, o_ref = refs[:-1], refs[-1]
    nblk = len(pages) // ppb
    n_heads, _, page_size = pages[0].shape[2:]
    for j in range(nblk):
        s = pages[j * ppb][0, 0]
        for q in range(1, ppb):
            s = s + pages[j * ppb + q][0, 0]
        o_ref[0, j] = jnp.sum(s, axis=-1) / (ppb * page_size)


def sample_block_means(cache_kt, layer, page_table):
    _, _, n_heads, d, page_size = cache_kt.shape
    db, n_pages = page_table.shape
    ppb = MOBA_BLOCK // page_size
    n_full = (n_pages * page_size) // MOBA_BLOCK
    pps = PAGES_PER_STEP
    assert n_pages % pps == 0 and pps % ppb == 0

    def page_spec(q):
        return pl.BlockSpec((1, 1, n_heads, d, page_size), lambda bi, g, pt: (layer, pt[bi, g * pps + q], 0, 0, 0))

    return pl.pallas_call(
        functools.partial(_sample_kmean_kernel, ppb=ppb),
        grid_spec=pltpu.PrefetchScalarGridSpec(
            num_scalar_prefetch=1,
            grid=(db, n_pages // pps),
            in_specs=[page_spec(q) for q in range(pps)],
            out_specs=pl.BlockSpec((1, pps // ppb, n_heads, d), lambda bi, g, pt: (bi, g, 0, 0)),
        ),
        out_shape=jax.ShapeDtypeStruct((db, n_full, n_heads, d), F32),
        compiler_params=_cparams("parallel", "arbitrary"),
        name="sample_block_means",
    )(page_table, *([cache_kt] * pps))


def _moba_sample_kernel(pt_ref, q_ref, kn_ref, vn_ref, km_ref, *refs, ppb, n_heads, n_sel):
    n_pages = (len(refs) - 5) // 2
    k_pages = refs[:n_pages]
    v_pages = refs[n_pages:2 * n_pages]
    o_ref = refs[2 * n_pages]
    sel_scr, m_scr, l_scr, acc_scr = refs[2 * n_pages + 1:]
    n = pl.program_id(1)
    q32 = q_ref[0]
    q = (q32 * HEAD_DIM ** -0.5).astype(BF16)
    page_size = k_pages[0].shape[4]

    @pl.when(n == 0)
    def _():
        if n_sel > 0:
            gate = _bdot(q32, km_ref[0], _NT, precision=HI)
            sel_scr[...] = _top_blocks(gate, jnp.ones(gate.shape, jnp.bool_), n_sel)
        s = _bdot(q, kn_ref[0], _NT)
        q_t = lax.broadcasted_iota(jnp.int32, s.shape, 1)
        k_t = lax.broadcasted_iota(jnp.int32, s.shape, 2)
        s = jnp.where(k_t <= q_t, s, NEG_INF)
        m = jnp.max(s, axis=2, keepdims=True)
        p = jnp.exp(s - m)
        m_scr[...] = m
        l_scr[...] = jnp.sum(p, axis=2, keepdims=True)
        acc_scr[...] = _bdot(p, vn_ref[0], _NN)

    if n_sel > 0:
        sel = sel_scr[...]
        blk = lax.broadcasted_iota(jnp.int32, sel.shape, 2)
        scores = []
        for pg in range(n_pages):
            b = n * (n_pages // ppb) + pg // ppb
            chosen = jnp.sum(jnp.where(blk == b, sel, 0.0), axis=2, keepdims=True) > 0.0
            scores.append(jnp.where(chosen, _bdot(q, k_pages[pg][0, 0], _NN), NEG_INF))
        s = jnp.concatenate(scores, axis=2)
        m = m_scr[...]
        m_new = jnp.maximum(m, jnp.max(s, axis=2, keepdims=True))
        alpha = jnp.exp(m - m_new)
        p = jnp.exp(s - m_new)
        m_scr[...] = m_new
        l_scr[...] = alpha * l_scr[...] + jnp.sum(p, axis=2, keepdims=True)
        acc = alpha * acc_scr[...]
        for pg in range(n_pages):
            acc = acc + _bdot(p[:, :, pg * page_size:(pg + 1) * page_size], v_pages[pg][0, 0], _NT)
        acc_scr[...] = acc

    @pl.when(n == pl.num_programs(1) - 1)
    def _():
        o_ref[0] = acc_scr[...] / l_scr[...]


def moba_sample(cache_kt, cache_vt, layer, page_table, q_rows, k_rows, v_rows, kmean):
    _, _, n_heads, d, page_size = cache_kt.shape
    db, t, _ = q_rows.shape
    n_pages = page_table.shape[1]
    ppb = MOBA_BLOCK // page_size
    n_full = (n_pages * page_size) // MOBA_BLOCK
    assert n_full * ppb == n_pages, "past rows inside the current block are not supported"
    n_sel = min(MOBA_TOPK, n_full)
    tp = -(-t // SUBLANES) * SUBLANES
    hm = lambda x: jnp.pad(_to_heads(x), ((0, 0), (0, 0), (0, tp - t), (0, 0)))
    new_spec = pl.BlockSpec((1, n_heads, tp, d), lambda bi, n, pt: (bi, 0, 0, 0))

    pps = SAMPLE_BLOCKS_PER_STEP * ppb
    assert n_pages % pps == 0

    def page_spec(q):
        return pl.BlockSpec((1, 1, n_heads, d, page_size), lambda bi, n, pt: (layer, pt[bi, n * pps + q], 0, 0, 0))

    o = pl.pallas_call(
        functools.partial(_moba_sample_kernel, ppb=ppb, n_heads=n_heads, n_sel=n_sel),
        grid_spec=pltpu.PrefetchScalarGridSpec(
            num_scalar_prefetch=1,
            grid=(db, n_pages // pps),
            in_specs=[new_spec, new_spec, new_spec,
                      pl.BlockSpec((1, n_heads, n_full, d), lambda bi, n, pt: (bi, 0, 0, 0))]
            + [page_spec(q) for q in range(pps)] * 2,
            out_specs=new_spec,
            scratch_shapes=[
                pltpu.VMEM((n_heads, tp, n_full), F32),
                pltpu.VMEM((n_heads, tp, 1), F32),
                pltpu.VMEM((n_heads, tp, 1), F32),
                pltpu.VMEM((n_heads, tp, d), F32),
            ],
        ),
        out_shape=jax.ShapeDtypeStruct((db, n_heads, tp, d), F32),
        compiler_params=_cparams("parallel", "arbitrary"),
        name="moba_sample",
    )(page_table, hm(q_rows), hm(k_rows), hm(v_rows), kmean, *([cache_kt] * pps), *([cache_vt] * pps))
    return o[:, :, :t]


def _mix_out_kernel(y_ref, r_ref, k_ref, v_ref, g_ref, om_ref, x_ref, wo_ref, lnw_ref, lnb_ref, rk_ref, o_ref,
                    *, h_rwkv, h_moba):
    pieces = []
    for h in range(h_rwkv):
        y = y_ref[0, h]
        mean = jnp.mean(y, axis=-1, keepdims=True)
        yc = y - mean
        var = jnp.mean(yc * yc, axis=-1, keepdims=True)
        yn = yc * lax.rsqrt(var + GN_EPS) * lnw_ref[h:h + 1, :] + lnb_ref[h:h + 1, :]
        bonus = jnp.sum(r_ref[0, h] * k_ref[0, h] * rk_ref[h:h + 1, :], axis=-1, keepdims=True) * v_ref[0, h]
        pieces.append((yn + bonus) * g_ref[0, h])
    pieces += [om_ref[0, h] for h in range(h_moba)]
    o = jnp.concatenate(pieces, axis=1).astype(BF16)
    o_ref[0] = x_ref[0] + jnp.dot(o, wo_ref[...], preferred_element_type=F32)


def mix_out(y, r, k, v, g, o_moba, x, w_out, ln_w, ln_b, r_k, *, tt_target=512):
    b, h_rwkv, t, n = y.shape
    h_moba = o_moba.shape[1]
    d = x.shape[2]
    tt = _row_tile(t, tt_target)
    hm_r = pl.BlockSpec((1, h_rwkv, tt, n), lambda bi, i: (bi, 0, i, 0))
    hm_m = pl.BlockSpec((1, h_moba, tt, n), lambda bi, i: (bi, 0, i, 0))
    full = lambda shape: pl.BlockSpec(shape, lambda bi, i: (0,) * len(shape))
    params = [w_out, ln_w.reshape(h_rwkv, n), ln_b.reshape(h_rwkv, n), r_k.reshape(h_rwkv, n)]
    return pl.pallas_call(
        functools.partial(_mix_out_kernel, h_rwkv=h_rwkv, h_moba=h_moba),
        grid=(b, t // tt),
        in_specs=[hm_r] * 5 + [hm_m, pl.BlockSpec((1, tt, d), lambda bi, i: (bi, i, 0))] + [full(p.shape) for p in params],
        out_specs=pl.BlockSpec((1, tt, d), lambda bi, i: (bi, i, 0)),
        out_shape=jax.ShapeDtypeStruct((b, t, d), F32),
        compiler_params=_cparams("parallel", "parallel"),
        name="mix_out",
    )(y, r, k, v, g, o_moba, x, *params)


def _to_heads(x_rows):
    b, t, dm = x_rows.shape
    return x_rows.reshape(b, t, dm // HEAD_DIM, HEAD_DIM).transpose(0, 2, 1, 3)


def _mixer_front(y, lw, p_prev_rows, wkv_prev, cos, sin, *, head_major):
    b, t, d = y.shape
    flat = y.reshape(b * t, d)
    pr = norm_proj(flat, lw['mix_norm'], lw['w_in_r']).reshape(b, t, -1)
    pm = norm_proj(flat, lw['mix_norm'], lw['w_in_m']).reshape(b, t, -1)
    shift = rms_norm_rows(y[:, -1], lw['mix_norm'])
    r, log_w, k, v, kk, kka, g, *cum = rwkv_prep(pr, p_prev_rows, lw)
    if cum:
        y_wkv, wkv_new = wkv_chunk_scan(r, log_w, cum[0], k, v, kk, kka, wkv_prev)
    else:
        y_wkv, wkv_new = wkv_scan(r, log_w, k, v, kk, kka, wkv_prev)
    k_rows, *moba_in = rope_qkv(pm, cos, sin, head_major=head_major)
    d_m = k_rows.shape[2]
    v_rows = pm[:, :, 2 * d_m:]
    return dict(r=r, k=k, v=v, g=g, y=y_wkv, wkv=wkv_new, shift=shift, k_rows=k_rows, v_rows=v_rows, moba_in=moba_in)


@jax.jit
def kernel(x_prompt, x_sample, cache_k, cache_v, state_wkv, state_shift, page_table, ffn1_norm, ffn1_w_gate,
           ffn1_w_up, ffn1_w_down, mix_norm, w_in, w_out, rwkv_mu, rwkv_w0, rwkv_w2, rwkv_a0, rwkv_a2, rwkv_g2,
           rwkv_k_k, rwkv_k_a, rwkv_r_k, rwkv_ln_w, rwkv_ln_b, ffn2_norm, ffn2_w_gate, ffn2_w_up, ffn2_w_down,
           final_norm):
    b, t, d = x_prompt.shape
    db, ts, _ = x_sample.shape
    depth = w_in.shape[0]
    d_r = rwkv_w0.shape[1]
    h_rwkv = d_r // HEAD_DIM
    c_r = rwkv_mu.shape[1]
    d_m = (w_in.shape[2] - c_r) // 3
    h_moba = d_m // HEAD_DIM
    past_len = page_table.shape[1] * cache_k.shape[2]
    ck = cache_k.transpose(0, 1, 3, 4, 2)
    cv = cache_v.transpose(0, 1, 3, 4, 2)
    cos_p, sin_p = rope_tables(jnp.arange(t), h_moba)
    cos_s, sin_s = rope_tables(past_len + jnp.arange(ts), h_moba)

    yp, ys = x_prompt, x_sample
    outs = {n: [] for n in ('kp', 'vp', 'ks', 'vs', 'wp', 'ws', 'sp', 'ss')}
    for l in range(depth):
        lw = {'mix_norm': mix_norm[l], 'w_in_r': w_in[l, :, :c_r].astype(BF16), 'w_in_m': w_in[l, :, c_r:].astype(BF16),
              'mu': rwkv_mu[l], 'w0': rwkv_w0[l], 'w2': rwkv_w2[l], 'a0': rwkv_a0[l], 'a2': rwkv_a2[l],
              'g2': rwkv_g2[l], 'k_k': rwkv_k_k[l], 'k_a': rwkv_k_a[l]}
        wo = w_out[l].astype(BF16)
        f1 = (ffn1_norm[l], ffn1_w_gate[l].astype(BF16), ffn1_w_up[l].astype(BF16), ffn1_w_down[l].astype(BF16))
        f2 = (ffn2_norm[l], ffn2_w_gate[l].astype(BF16), ffn2_w_up[l].astype(BF16), ffn2_w_down[l].astype(BF16))

        yp = ffn_half(yp.reshape(b * t, d), *f1).reshape(b, t, d)
        ys = ffn_half(ys.reshape(db * ts, d), *f1).reshape(db, ts, d)

        fp = _mixer_front(yp, lw, jnp.zeros((b, c_r), F32), jnp.zeros((b, h_rwkv, HEAD_DIM, HEAD_DIM), F32),
                          cos_p, sin_p, head_major=True)
        p_prev_s = norm_proj(state_shift[l], lw['mix_norm'], lw['w_in_r'], normalize=False)
        fs = _mixer_front(ys, lw, p_prev_s, state_wkv[l], cos_s, sin_s, head_major=False)

        kmean_p = _to_heads(block_means(fp['k_rows']))
        om_p = moba_prompt(*fp['moba_in'], kmean_p)
        kmean_s = sample_block_means(ck, l, page_table).transpose(0, 2, 1, 3)
        om_s = moba_sample(ck, cv, l, page_table, fs['moba_in'][0], fs['k_rows'], fs['v_rows'], kmean_s)

        mix = lambda f, om, y: mix_out(f['y'], f['r'], f['k'], f['v'], f['g'], om, y, wo,
                                       rwkv_ln_w[l], rwkv_ln_b[l], rwkv_r_k[l])
        yp = mix(fp, om_p, yp)
        ys = mix(fs, om_s, ys)

        yp = ffn_half(yp.reshape(b * t, d), *f2).reshape(b, t, d)
        ys = ffn_half(ys.reshape(db * ts, d), *f2).reshape(db, ts, d)

        outs['kp'].append(fp['k_rows'].reshape(b, t, h_moba, HEAD_DIM))
        outs['vp'].append(fp['v_rows'].reshape(b, t, h_moba, HEAD_DIM))
        outs['ks'].append(fs['k_rows'].reshape(db, ts, h_moba, HEAD_DIM))
        outs['vs'].append(fs['v_rows'].reshape(db, ts, h_moba, HEAD_DIM))
        outs['wp'].append(fp['wkv'])
        outs['ws'].append(fs['wkv'])
        outs['sp'].append(fp['shift'])
        outs['ss'].append(fs['shift'])

    y_prompt = rms_norm_rows(yp.reshape(b * t, d), final_norm).reshape(b, t, d)
    y_sample = rms_norm_rows(ys.reshape(db * ts, d), final_norm).reshape(db, ts, d)
    st = lambda n: jnp.stack(outs[n])
    return (y_prompt, y_sample, st('kp'), st('vp'), st('ks'), st('vs'), st('wp'), st('ws'), st('sp'), st('ss'))
```

```python
import functools
import math

import jax
import jax.numpy as jnp
from jax import lax
from jax.experimental import pallas as pl
from jax.experimental.pallas import tpu as pltpu

F32 = jnp.float32
BF16 = jnp.bfloat16
HI = lax.Precision.HIGHEST

HEAD_DIM = 64
MOBA_BLOCK = 256
MOBA_TOPK = 3
ROPE_THETA = 10000.0
NORM_EPS = 1e-6
GN_EPS = 64e-5
NEG_INF = -1e30
DECAY_LORA = 64
ICLR_LORA = 64
GATE_LORA = 128
WKV_CHUNKS_PER_ITER = 4
LANES = 128
WKV_CHUNK = 64

VMEM_LIMIT = 56 * 1024 * 1024


def _cparams(*sem):
    return pltpu.CompilerParams(dimension_semantics=sem, vmem_limit_bytes=VMEM_LIMIT)


def _row_tile(n, target):
    t = min(n, target)
    while n % t:
        t //= 2
    return t


def _nt_dot(a, b, precision=None):
    return lax.dot_general(a, b, (((1,), (1,)), ((), ())), precision=precision, preferred_element_type=F32)


def _ffn_kernel(x_ref, g_ref, wg_ref, wu_ref, wd_ref, o_ref, h_scr, acc_scr):
    j = pl.program_id(1)

    @pl.when(j == 0)
    def _():
        x = x_ref[...]
        ms = jnp.mean(x * x, axis=-1, keepdims=True)
        h_scr[...] = (x * lax.rsqrt(ms + NORM_EPS) * g_ref[...]).astype(BF16)
        acc_scr[...] = jnp.zeros_like(acc_scr)

    h = h_scr[...]
    a = jnp.dot(h, wg_ref[...], preferred_element_type=F32)
    b = jnp.dot(h, wu_ref[...], preferred_element_type=F32)
    u = a * jax.nn.sigmoid(a) * b
    acc_scr[...] += jnp.dot(u.astype(BF16), wd_ref[...], preferred_element_type=F32)

    @pl.when(j == pl.num_programs(1) - 1)
    def _():
        o_ref[...] = x_ref[...] + 0.5 * acc_scr[...]


def ffn_half(x, g, wg, wu, wd, *, tm_target=1024, tf=1408):
    n, d = x.shape
    f = wg.shape[1]
    tm = _row_tile(n, tm_target)
    assert f % tf == 0
    return pl.pallas_call(
        _ffn_kernel,
        grid=(n // tm, f // tf),
        in_specs=[
            pl.BlockSpec((tm, d), lambda i, j: (i, 0)),
            pl.BlockSpec((1, d), lambda i, j: (0, 0)),
            pl.BlockSpec((d, tf), lambda i, j: (0, j)),
            pl.BlockSpec((d, tf), lambda i, j: (0, j)),
            pl.BlockSpec((tf, d), lambda i, j: (j, 0)),
        ],
        out_specs=pl.BlockSpec((tm, d), lambda i, j: (i, 0)),
        out_shape=jax.ShapeDtypeStruct((n, d), F32),
        scratch_shapes=[pltpu.VMEM((tm, d), BF16), pltpu.VMEM((tm, d), F32)],
        compiler_params=_cparams("parallel", "arbitrary"),
        name="ffn_half",
    )(x, g.reshape(1, d), wg, wu, wd)


def _proj_kernel(x_ref, g_ref, w_ref, o_ref, h_scr, *, normalize):
    @pl.when(pl.program_id(1) == 0)
    def _():
        x = x_ref[...]
        if normalize:
            ms = jnp.mean(x * x, axis=-1, keepdims=True)
            x = x * lax.rsqrt(ms + NORM_EPS) * g_ref[...]
        h_scr[...] = x.astype(BF16)

    o_ref[...] = jnp.dot(h_scr[...], w_ref[...], preferred_element_type=F32)


def norm_proj(x, g, w, *, normalize=True, tm_target=1024):
    n, d = x.shape
    c = w.shape[1]
    tm = _row_tile(n, tm_target)
    tn = c
    return pl.pallas_call(
        functools.partial(_proj_kernel, normalize=normalize),
        grid=(n // tm, c // tn),
        in_specs=[
            pl.BlockSpec((tm, d), lambda i, j: (i, 0)),
            pl.BlockSpec((1, d), lambda i, j: (0, 0)),
            pl.BlockSpec((d, tn), lambda i, j: (0, j)),
        ],
        out_specs=pl.BlockSpec((tm, tn), lambda i, j: (i, j)),
        out_shape=jax.ShapeDtypeStruct((n, c), F32),
        scratch_shapes=[pltpu.VMEM((tm, d), BF16)],
        compiler_params=_cparams("parallel", "arbitrary"),
        name="norm_proj",
    )(x, g.reshape(1, d), w)


def _rms_kernel(x_ref, g_ref, o_ref):
    x = x_ref[...]
    ms = jnp.mean(x * x, axis=-1, keepdims=True)
    o_ref[...] = x * lax.rsqrt(ms + NORM_EPS) * g_ref[...]


def rms_norm_rows(x, g, *, tm_target=1024):
    n, d = x.shape
    tm = _row_tile(n, tm_target)
    return pl.pallas_call(
        _rms_kernel,
        grid=(n // tm,),
        in_specs=[pl.BlockSpec((tm, d), lambda i: (i, 0)), pl.BlockSpec((1, d), lambda i: (0, 0))],
        out_specs=pl.BlockSpec((tm, d), lambda i: (i, 0)),
        out_shape=jax.ShapeDtypeStruct((n, d), F32),
        compiler_params=_cparams("parallel"),
        name="rms_norm",
    )(x, g.reshape(1, d))


def _head_block_diag(n):
    r = lax.broadcasted_iota(jnp.int32, (n, n), 0) // HEAD_DIM
    c = lax.broadcasted_iota(jnp.int32, (n, n), 1) // HEAD_DIM
    return (r == c).astype(F32)


def _prep_kernel(*refs, has_prev_tile, with_cum, n_heads):
    if has_prev_tile:
        p_ref, prev8_ref, pprev_ref = refs[:3]
        rest = refs[3:]
    else:
        p_ref, pprev_ref = refs[:2]
        prev8_ref = None
        rest = refs[2:]
    mu_ref, w0_ref, w2_ref, a0_ref, a2_ref, g2_ref, kk_ref, ka_ref = rest[:8]
    r_out, w_out, k_out, v_out, kk_out, kka_out, g_out = rest[8:15]
    c_out = rest[15] if with_cum else None
    d_r = n_heads * HEAD_DIM
    p = p_ref[0]
    tt = p.shape[0]
    prev_row = pprev_ref[0]
    if has_prev_tile:
        prev_row = jnp.where(pl.program_id(1) == 0, prev_row, prev8_ref[0, 7:8, :])
    row = lax.broadcasted_iota(jnp.int32, p.shape, 0)
    if tt % 8 == 0:
        rolled = pltpu.roll(p, 1, 0)
    else:
        rolled = jnp.concatenate([p[tt - 1:], p[:tt - 1]], axis=0)
    p_shift = jnp.where(row == 0, prev_row, rolled)
    xm = p + (p_shift - p) * mu_ref[...]
    r = xm[:, :d_r]
    k = xm[:, d_r:2 * d_r]
    v = xm[:, 2 * d_r:3 * d_r]
    c0 = 3 * d_r
    xw = xm[:, c0:c0 + DECAY_LORA]
    xa = xm[:, c0 + DECAY_LORA:c0 + DECAY_LORA + ICLR_LORA]
    xg = xm[:, c0 + DECAY_LORA + ICLR_LORA:]
    z = -(w0_ref[...] + _dot_split(jnp.tanh(xw), w2_ref[...]))
    softplus = jnp.maximum(z, 0.0) + jnp.log(1.0 + jnp.exp(-jnp.abs(z)))
    log_decay = -jnp.exp(-softplus - 0.5)
    if with_cum:
        ci = lax.broadcasted_iota(jnp.int32, (tt, tt), 0)
        cj = lax.broadcasted_iota(jnp.int32, (tt, tt), 1)
        tri = ((ci // WKV_CHUNK == cj // WKV_CHUNK) & (cj <= ci)).astype(BF16)
        cum = sum(jnp.dot(tri, piece, preferred_element_type=F32) for piece in _split3(log_decay))
    a = jax.nn.sigmoid(a0_ref[...] + _dot_split(xa, a2_ref[...]))
    g = _dot_split(jax.nn.sigmoid(xg), g2_ref[...])
    kk = k * kk_ref[...]
    pair_diag = _head_block_diag(LANES).astype(BF16)
    sq = kk * kk
    ss = jnp.concatenate(
        [sum(jnp.dot(piece, pair_diag, preferred_element_type=F32) for piece in _split3(sq[:, c:c + LANES]))
         for c in range(0, d_r, LANES)], axis=1)
    kk = kk / jnp.maximum(jnp.sqrt(ss), 1e-12)
    k2 = k * (1.0 + (a - 1.0) * ka_ref[...])
    kka = kk * a
    for h in range(n_heads):
        sl = slice(h * HEAD_DIM, (h + 1) * HEAD_DIM)
        r_out[0, h] = r[:, sl]
        w_out[0, h] = log_decay[:, sl]
        if with_cum:
            c_out[0, h] = cum[:, sl]
        k_out[0, h] = k2[:, sl]
        v_out[0, h] = v[:, sl]
        kk_out[0, h] = kk[:, sl]
        kka_out[0, h] = kka[:, sl]
        g_out[0, h] = g[:, sl]


def rwkv_prep(pr, p_prev, lp, *, tt_target=256):
    b, t, c = pr.shape
    d_r = lp['w0'].shape[0]
    n_heads = d_r // HEAD_DIM
    tt = _row_tile(t, tt_target)
    nt = t // tt
    has_prev = nt > 1
    with_cum = tt % WKV_CHUNK == 0
    n_out = 8 if with_cum else 7
    row = lambda x: x.reshape(1, -1)
    full = lambda shape: pl.BlockSpec(shape, lambda bi, i: (0,) * len(shape))
    in_specs = [pl.BlockSpec((1, tt, c), lambda bi, i: (bi, i, 0))]
    args = [pr]
    if has_prev:
        in_specs.append(pl.BlockSpec((1, 8, c), lambda bi, i: (bi, jnp.maximum(i * (tt // 8) - 1, 0), 0)))
        args.append(pr)
    in_specs.append(pl.BlockSpec((1, 1, c), lambda bi, i: (bi, 0, 0)))
    args.append(p_prev.reshape(b, 1, c))
    params = [row(lp['mu']), row(lp['w0']), lp['w2'], row(lp['a0']), lp['a2'], lp['g2'], row(lp['k_k']), row(lp['k_a'])]
    in_specs += [full(x.shape) for x in params]
    args += params
    hm = jax.ShapeDtypeStruct((b, n_heads, t, HEAD_DIM), F32)
    hm_spec = pl.BlockSpec((1, n_heads, tt, HEAD_DIM), lambda bi, i: (bi, 0, i, 0))
    return pl.pallas_call(
        functools.partial(_prep_kernel, has_prev_tile=has_prev, with_cum=with_cum, n_heads=n_heads),
        grid=(b, nt),
        in_specs=in_specs,
        out_specs=[hm_spec] * n_out,
        out_shape=[hm] * n_out,
        compiler_params=_cparams("parallel", "arbitrary"),
        name="rwkv_prep",
    )(*args)


def _scan_kernel(r_ref, lw_ref, k_ref, v_ref, kk_ref, kka_ref, s0_ref, y_ref, s_out_ref, s_scr, *, n_heads, tc):
    @pl.when(pl.program_id(1) == 0)
    def _():
        s_scr[...] = s0_ref[0]

    eye = (lax.broadcasted_iota(jnp.int32, (HEAD_DIM, HEAD_DIM), 0)
           == lax.broadcasted_iota(jnp.int32, (HEAD_DIM, HEAD_DIM), 1))

    def body(t, carry):
        for h in range(n_heads):
            s = s_scr[h]
            row = lambda ref: ref[0, h, pl.ds(t, 1), :]
            sa = -jnp.sum(s * row(kk_ref), axis=1, keepdims=True)
            v_col = jnp.sum(jnp.where(eye, row(v_ref), 0.0), axis=1, keepdims=True)
            s = s * jnp.exp(row(lw_ref)) + sa * row(kka_ref) + v_col * row(k_ref)
            s_scr[h] = s
            y_col = jnp.sum(s * row(r_ref), axis=1, keepdims=True)
            y_ref[0, h, pl.ds(t, 1), :] = jnp.sum(jnp.where(eye, y_col, 0.0), axis=0, keepdims=True)
        return carry

    lax.fori_loop(0, tc, body, 0)
    s_out_ref[0] = s_scr[...]


def _bdot(a, b, contract, precision=None):
    if precision is None:
        a, b = a.astype(BF16), b.astype(BF16)
    return lax.dot_general(a, b, (((contract[0],), (contract[1],)), ((0,), (0,))), precision=precision,
                           preferred_element_type=F32)


_NN, _NT, _TN = (2, 1), (2, 2), (1, 1)


def _split(x):
    hi = x.astype(BF16)
    return hi, (x - hi.astype(F32)).astype(BF16)


def _split3(x):
    hi = x.astype(BF16)
    rest = x - hi.astype(F32)
    mid = rest.astype(BF16)
    return hi, mid, (rest - mid.astype(F32)).astype(BF16)


def _dot_split(a, b):
    (ah, al), (bh, bl) = _split(a), _split(b)
    dot = lambda x, y: jnp.dot(x, y, preferred_element_type=F32)
    return dot(ah, bh) + (dot(ah, bl) + dot(al, bh))


def _bdot_split(a, b, contract):
    (ah, al), (bh, bl) = a, b
    return _bdot(ah, bh, contract) + (_bdot(ah, bl, contract) + _bdot(al, bh, contract))


def _unit_lower_inverse(t_strict):
    n = t_strict.shape[1]
    ri = lax.broadcasted_iota(jnp.int32, (n, n), 0)
    ci = lax.broadcasted_iota(jnp.int32, (n, n), 1)
    same = lambda size: (ri // size) == (ci // size)
    mm = lambda x, y: _bdot_split(x, y, _NN)
    neg32 = jnp.where(same(8), -t_strict, 0.0)
    neg = _split(neg32)
    n2 = _split(mm(neg, neg))
    n4 = _split(mm(n2, n2))
    inv = (ri == ci).astype(F32) + neg32
    inv = inv + mm(_split(inv), n2)
    inv = inv + mm(_split(inv), n4)
    for size in (16, 32, 64):
        off = _split(jnp.where(same(size) & jnp.logical_not(same(size // 2)), t_strict, 0.0))
        inv_s = _split(inv)
        inv = inv - mm(inv_s, _split(mm(off, inv_s)))
    return inv


def _chunk_scan_kernel(r_ref, lw_ref, c_ref, k_ref, v_ref, kk_ref, kka_ref, s0_ref, y_ref, s_out_ref, s_scr,
                       *, n_heads, n_chunks):
    L = WKV_CHUNK

    @pl.when(pl.program_id(1) == 0)
    def _():
        s_scr[...] = s0_ref[0]

    ri = lax.broadcasted_iota(jnp.int32, (2 * L, 2 * L), 0)
    ci = lax.broadcasted_iota(jnp.int32, (2 * L, 2 * L), 1)
    keep = ((ri < L) & ((ci % L) < ri)) | ((ri >= L) & ((ci % L) <= (ri - L)))

    G = WKV_CHUNKS_PER_ITER

    def body(cidx, carry):
        rows = pl.ds(pl.multiple_of(cidx * (G * L), G * L), G * L)
        ld = lambda ref: ref[0, :, rows, :].reshape(n_heads * G, L, HEAD_DIM)
        r, lw, c, k, v, a, b = (ld(x) for x in (r_ref, lw_ref, c_ref, k_ref, v_ref, kk_ref, kka_ref))
        c_last = c[:, L - 1:L, :]
        inv_g = jnp.exp(-c)
        a_t = a * jnp.exp(c - lw)
        b_t = b * inv_g
        k_t = k * inv_g
        r_t = r * jnp.exp(c)
        to_end = jnp.exp(c_last - c)
        b_end = b * to_end
        k_end = k * to_end
        coef = _bdot_split(_split(jnp.concatenate([a_t, r_t], axis=1)),
                           _split(jnp.concatenate([b_t, k_t], axis=1)), _NT)
        coef = jnp.where(keep, coef, 0.0)
        t_ab, t_ak = coef[:, :L, :L], coef[:, :L, L:]
        a_rb, a_rk = coef[:, L:, :L], coef[:, L:, L:]
        m_inv = _unit_lower_inverse(t_ab)
        w_mat = _bdot(m_inv, a_t, _NN)
        u_mat = -_bdot(m_inv, _bdot(t_ak, v, _NN), _NN)
        y_intra = _bdot(a_rb, u_mat, _NN) + _bdot(a_rk, v, _NN)
        r_w = r_t - _bdot(a_rb, w_mat, _NN)
        h_mat = _bdot(u_mat, b_end, _TN) + _bdot(v, k_end, _TN)
        per_chunk = lambda x: x.reshape(n_heads, G, *x.shape[1:])
        r_w, y_intra, w_mat, b_end, h_mat, decay_end = (
            per_chunk(x) for x in (r_w, y_intra, w_mat, b_end, h_mat, jnp.exp(c_last)))
        s0 = s_scr[...]
        ys = []
        for g in range(G):
            ys.append(_bdot(r_w[:, g], s0, _NT) + y_intra[:, g])
            p = _bdot(s0, w_mat[:, g], _NT)
            s0 = s0 * decay_end[:, g] - _bdot(p, b_end[:, g], _NN) + h_mat[:, g]
        y_ref[0, :, rows, :] = jnp.concatenate(ys, axis=1)
        s_scr[...] = s0
        return carry

    lax.fori_loop(0, n_chunks // G, body, 0)
    s_out_ref[0] = s_scr[...]


def wkv_chunk_scan(r, lw, c, k, v, kk, kka, s0, *, tc_target=256):
    b, h, t, n = r.shape
    tc = _row_tile(t, tc_target)
    assert tc % (WKV_CHUNK * WKV_CHUNKS_PER_ITER) == 0 and n == WKV_CHUNK
    seq = pl.BlockSpec((1, h, tc, n), lambda bi, i: (bi, 0, i, 0))
    st = pl.BlockSpec((1, h, n, n), lambda bi, i: (bi, 0, 0, 0))
    return pl.pallas_call(
        functools.partial(_chunk_scan_kernel, n_heads=h, n_chunks=tc // WKV_CHUNK),
        grid=(b, t // tc),
        in_specs=[seq] * 7 + [st],
        out_specs=[seq, st],
        out_shape=[jax.ShapeDtypeStruct((b, h, t, n), F32), jax.ShapeDtypeStruct((b, h, n, n), F32)],
        scratch_shapes=[pltpu.VMEM((h, n, n), F32)],
        compiler_params=_cparams("parallel", "arbitrary"),
        name="wkv_chunk_scan",
    )(r, lw, c, k, v, kk, kka, s0)


def wkv_scan(r, w, k, v, kk, kka, s0, *, tc_target=256):
    b, h, t, n = r.shape
    tc = _row_tile(t, tc_target)
    seq = pl.BlockSpec((1, h, tc, n), lambda bi, c: (bi, 0, c, 0))
    st = pl.BlockSpec((1, h, n, n), lambda bi, c: (bi, 0, 0, 0))
    return pl.pallas_call(
        functools.partial(_scan_kernel, n_heads=h, tc=tc),
        grid=(b, t // tc),
        in_specs=[seq] * 6 + [st],
        out_specs=[seq, st],
        out_shape=[jax.ShapeDtypeStruct((b, h, t, n), F32), jax.ShapeDtypeStruct((b, h, n, n), F32)],
        scratch_shapes=[pltpu.VMEM((h, n, n), F32)],
        compiler_params=_cparams("parallel", "arbitrary"),
        name="wkv_scan",
    )(r, w, k, v, kk, kka, s0)


def _rope_kernel(p_ref, cos_ref, sin_ref, krow_ref, q_out, k_out=None, vt_out=None, *, n_heads, head_major):
    d_m = n_heads * HEAD_DIM
    half = HEAD_DIM // 2
    cos = cos_ref[...]
    sin = sin_ref[...]
    first_half = (lax.broadcasted_iota(jnp.int32, cos.shape, 1) % HEAD_DIM) < half

    def rot(x):
        if x.shape[0] % 8 == 0:
            fwd = pltpu.roll(x, d_m - half, 1)
            bwd = pltpu.roll(x, half, 1)
        else:
            fwd = jnp.concatenate([x[:, half:], x[:, :half]], axis=1)
            bwd = jnp.concatenate([x[:, d_m - half:], x[:, :d_m - half]], axis=1)
        return x * cos + jnp.where(first_half, fwd, bwd) * sin

    p = p_ref[0]
    q = rot(p[:, :d_m])
    k = rot(p[:, d_m:2 * d_m])
    krow_ref[0] = k
    if not head_major:
        q_out[0] = q
        return
    v = p[:, 2 * d_m:]
    for h in range(n_heads):
        sl = slice(h * HEAD_DIM, (h + 1) * HEAD_DIM)
        q_out[0, h] = q[:, sl]
        k_out[0, h] = k[:, sl].astype(BF16)
        for jb in range(p.shape[0] // MOBA_BLOCK):
            vt = v[jb * MOBA_BLOCK:(jb + 1) * MOBA_BLOCK, sl].T
            ones = jnp.ones((VT_ROWS - HEAD_DIM, MOBA_BLOCK), F32)
            vt_out[0, h, jb] = jnp.concatenate([vt, ones], axis=0).astype(BF16)


def rope_qkv(pm, cos, sin, *, head_major, tt_target=512):
    b, t, c = pm.shape
    d_m = c // 3
    n_heads = d_m // HEAD_DIM
    tt = _row_tile(t, tt_target)
    rows_spec = pl.BlockSpec((1, tt, d_m), lambda bi, i: (bi, i, 0))
    rows = jax.ShapeDtypeStruct((b, t, d_m), F32)
    if head_major:
        assert tt % MOBA_BLOCK == 0
        hm_spec = pl.BlockSpec((1, n_heads, tt, HEAD_DIM), lambda bi, i: (bi, 0, i, 0))
        hm = lambda dt: jax.ShapeDtypeStruct((b, n_heads, t, HEAD_DIM), dt)
        nbt = tt // MOBA_BLOCK
        out_specs = [rows_spec, hm_spec, hm_spec,
                     pl.BlockSpec((1, n_heads, nbt, VT_ROWS, MOBA_BLOCK), lambda bi, i: (bi, 0, i, 0, 0))]
        out_shape = [rows, hm(F32), hm(BF16),
                     jax.ShapeDtypeStruct((b, n_heads, t // MOBA_BLOCK, VT_ROWS, MOBA_BLOCK), BF16)]
    else:
        out_specs = [rows_spec, rows_spec]
        out_shape = [rows, rows]
    return pl.pallas_call(
        functools.partial(_rope_kernel, n_heads=n_heads, head_major=head_major),
        grid=(b, t // tt),
        in_specs=[
            pl.BlockSpec((1, tt, c), lambda bi, i: (bi, i, 0)),
            pl.BlockSpec((tt, d_m), lambda bi, i: (i, 0)),
            pl.BlockSpec((tt, d_m), lambda bi, i: (i, 0)),
        ],
        out_specs=out_specs,
        out_shape=out_shape,
        compiler_params=_cparams("parallel", "parallel"),
        name="rope_qkv",
    )(pm, cos, sin)


def rope_tables(pos, n_heads):
    half = HEAD_DIM // 2
    inv_freq = ROPE_THETA ** (-jnp.arange(half, dtype=F32) / half)
    ang = pos.astype(F32)[:, None] * inv_freq[None, :]
    cos = jnp.cos(ang)
    sin = jnp.sin(ang)
    cos = jnp.tile(jnp.concatenate([cos, cos], axis=1), (1, n_heads))
    sin = jnp.tile(jnp.concatenate([-sin, sin], axis=1), (1, n_heads))
    return cos, sin


def _kmean_kernel(k_ref, o_ref, *, nblk):
    for j in range(nblk):
        o_ref[0, j:j + 1, :] = jnp.mean(k_ref[0, j * MOBA_BLOCK:(j + 1) * MOBA_BLOCK, :], axis=0, keepdims=True)


def block_means(k_rows):
    b, t, d_m = k_rows.shape
    nb = t // MOBA_BLOCK
    nblk = 8 if nb % 8 == 0 else nb
    return pl.pallas_call(
        functools.partial(_kmean_kernel, nblk=nblk),
        grid=(b, nb // nblk),
        in_specs=[pl.BlockSpec((1, nblk * MOBA_BLOCK, d_m), lambda bi, i: (bi, i, 0))],
        out_specs=pl.BlockSpec((1, nblk, d_m), lambda bi, i: (bi, i, 0)),
        out_shape=jax.ShapeDtypeStruct((b, nb, d_m), F32),
        compiler_params=_cparams("parallel", "parallel"),
        name="block_means",
    )(k_rows)


def _top_blocks(gate, valid, n_sel):
    ax = gate.ndim - 1
    nb = gate.shape[ax]
    blk = lax.broadcasted_iota(jnp.int32, gate.shape, ax)
    avail = valid
    sel = jnp.zeros(gate.shape, F32)
    for _ in range(n_sel):
        g = jnp.where(avail, gate, -jnp.inf)
        m = jnp.max(g, axis=ax, keepdims=True)
        first = jnp.min(jnp.where((g == m) & avail, blk, nb), axis=ax, keepdims=True)
        pick = blk == first
        sel = jnp.where(pick, 1.0, sel)
        avail = avail & jnp.logical_not(pick)
    return sel


def _top_blocks_t(gate, valid, n_sel):
    ax = gate.ndim - 2
    nb = gate.shape[ax]
    blk = lax.broadcasted_iota(jnp.int32, gate.shape, ax)
    avail = jnp.broadcast_to(valid, gate.shape)
    sel = jnp.zeros(gate.shape, F32)
    for _ in range(n_sel):
        g = jnp.where(avail, gate, -jnp.inf)
        m = jnp.max(g, axis=ax, keepdims=True)
        first = jnp.min(jnp.where((g == m) & avail, blk, nb), axis=ax, keepdims=True)
        pick = blk == first
        sel = jnp.where(pick, 1.0, sel)
        avail = avail & jnp.logical_not(pick)
    return sel


MOBA_HEADS_PER_STEP = 4
SUBLANES = 8
MOBA_BLOCKS_PER_ITER = SUBLANES // 2
VT_ROWS = HEAD_DIM + 2 * SUBLANES


def _moba_prompt_kernel(q_ref, k_ref, vt_ref, km_ref, o_ref, sel_scr, *, n_sel):
    i = pl.program_id(2)
    hb = q_ref.shape[1]
    nb = km_ref.shape[2]
    q32 = q_ref[0]
    q = (q32 * (HEAD_DIM ** -0.5 * math.log2(math.e))).astype(BF16)

    def scores_t(j0, n_blocks):
        rows = pl.ds(pl.multiple_of(j0 * MOBA_BLOCK, MOBA_BLOCK), n_blocks * MOBA_BLOCK)
        return _bdot(k_ref[0, :, rows, :], q, _NT)

    def weighted_values_t(j0, p):
        out = None
        for jb in range(p.shape[1] // MOBA_BLOCK):
            pv = _bdot(vt_ref[0, :, j0 + jb], p[:, jb * MOBA_BLOCK:(jb + 1) * MOBA_BLOCK], _NN)
            out = pv if out is None else out + pv
        return out

    key_pos = lax.broadcasted_iota(jnp.int32, (MOBA_BLOCK, MOBA_BLOCK), 0)
    q_pos = lax.broadcasted_iota(jnp.int32, (MOBA_BLOCK, MOBA_BLOCK), 1)
    s = jnp.where(key_pos <= q_pos, scores_t(i, 1), NEG_INF)
    m = jnp.max(s, axis=1, keepdims=True)
    p = jnp.exp2(s - m)
    acc = weighted_values_t(i, p)

    if n_sel > 0:
        gate_t = _bdot(km_ref[0], q32, _NT, precision=HI)
        blk = lax.broadcasted_iota(jnp.int32, (nb, MOBA_BLOCK), 0)
        sel_scr[...] = _top_blocks_t(gate_t, blk < i, n_sel)
        kb = MOBA_BLOCKS_PER_ITER

        def body(jj, carry):
            m, acc = carry
            j0 = jj * kb
            sel8 = sel_scr[:, pl.ds(pl.multiple_of((jj // 2) * 8, 8), 8), :]
            chosen = jnp.where(jj % 2 == 0, sel8[:, :kb], sel8[:, kb:]) > 0.0
            s = scores_t(j0, kb).reshape(hb, kb, MOBA_BLOCK, MOBA_BLOCK)
            blk_max = jnp.where(chosen, jnp.max(s, axis=2), NEG_INF)
            m_new = jnp.maximum(m, jnp.max(blk_max, axis=1, keepdims=True))
            alpha = jnp.exp2(m - m_new)
            p = jnp.exp2(s - m_new[:, :, None, :])
            acc = alpha * acc
            for jb in range(kb):
                pv = _bdot(vt_ref[0, :, j0 + jb], p[:, jb], _NN)
                acc = acc + jnp.where(chosen[:, jb:jb + 1], pv, 0.0)
            return m_new, acc

        m, acc = lax.fori_loop(0, (i + kb - 1) // kb, body, (m, acc))

    o_t = acc[:, :HEAD_DIM] / acc[:, HEAD_DIM:HEAD_DIM + 1]
    for h in range(hb):
        o_ref[0, h] = o_t[h].T


def moba_prompt(q, k, vt, kmean):
    b, h, t, d = q.shape
    assert t % MOBA_BLOCK == 0
    nb = t // MOBA_BLOCK
    n_sel = min(MOBA_TOPK, nb - 1)
    nb_pad = -(-nb // 8) * 8
    kmean = jnp.pad(kmean, ((0, 0), (0, 0), (0, nb_pad - nb), (0, 0)))
    hb = MOBA_HEADS_PER_STEP
    assert h % hb == 0 and nb % MOBA_BLOCKS_PER_ITER == 0
    q_spec = pl.BlockSpec((1, hb, MOBA_BLOCK, d), lambda bi, hi, i: (bi, hi, i, 0))
    return pl.pallas_call(
        functools.partial(_moba_prompt_kernel, n_sel=n_sel),
        grid=(b, h // hb, nb),
        in_specs=[
            q_spec,
            pl.BlockSpec((1, hb, t, d), lambda bi, hi, i: (bi, hi, 0, 0)),
            pl.BlockSpec((1, hb, nb, VT_ROWS, MOBA_BLOCK), lambda bi, hi, i: (bi, hi, 0, 0, 0)),
            pl.BlockSpec((1, hb, nb_pad, d), lambda bi, hi, i: (bi, hi, 0, 0)),
        ],
        out_specs=q_spec,
        out_shape=jax.ShapeDtypeStruct((b, h, t, d), F32),
        scratch_shapes=[pltpu.VMEM((hb, nb_pad, MOBA_BLOCK), F32)],
        compiler_params=_cparams("parallel", "parallel", "arbitrary"),
        name="moba_prompt",
    )(q, k, vt, kmean)


PAGES_PER_STEP = 16
SAMPLE_BLOCKS_PER_STEP = 16


def _sample_kmean_kernel(pt_ref, *refs, ppb):
    pages, o_ref = refs[:-1], refs[-1]
    nblk = len(pages) // ppb
    n_heads, _, page_size = pages[0].shape[2:]
    for j in range(nblk):
        s = pages[j * ppb][0, 0]
        for q in range(1, ppb):
            s = s + pages[j * ppb + q][0, 0]
        o_ref[0, j] = jnp.sum(s, axis=-1) / (ppb * page_size)


def sample_block_means(cache_kt, layer, page_table):
    _, _, n_heads, d, page_size = cache_kt.shape
    db, n_pages = page_table.shape
    ppb = MOBA_BLOCK // page_size
    n_full = (n_pages * page_size) // MOBA_BLOCK
    pps = PAGES_PER_STEP
    assert n_pages % pps == 0 and pps % ppb == 0

    def page_spec(q):
        return pl.BlockSpec((1, 1, n_heads, d, page_size), lambda bi, g, pt: (layer, pt[bi, g * pps + q], 0, 0, 0))

    return pl.pallas_call(
        functools.partial(_sample_kmean_kernel, ppb=ppb),
        grid_spec=pltpu.PrefetchScalarGridSpec(
            num_scalar_prefetch=1,
            grid=(db, n_pages // pps),
            in_specs=[page_spec(q) for q in range(pps)],
            out_specs=pl.BlockSpec((1, pps // ppb, n_heads, d), lambda bi, g, pt: (bi, g, 0, 0)),
        ),
        out_shape=jax.ShapeDtypeStruct((db, n_full, n_heads, d), F32),
        compiler_params=_cparams("parallel", "arbitrary"),
        name="sample_block_means",
    )(page_table, *([cache_kt] * pps))


def _moba_sample_kernel(pt_ref, q_ref, kn_ref, vn_ref, km_ref, *refs, ppb, n_heads, n_sel):
    n_pages = (len(refs) - 5) // 2
    k_pages = refs[:n_pages]
    v_pages = refs[n_pages:2 * n_pages]
    o_ref = refs[2 * n_pages]
    sel_scr, m_scr, l_scr, acc_scr = refs[2 * n_pages + 1:]
    n = pl.program_id(1)
    q32 = q_ref[0]
    q = (q32 * HEAD_DIM ** -0.5).astype(BF16)
    page_size = k_pages[0].shape[4]

    @pl.when(n == 0)
    def _():
        if n_sel > 0:
            gate = _bdot(q32, km_ref[0], _NT, precision=HI)
            sel_scr[...] = _top_blocks(gate, jnp.ones(gate.shape, jnp.bool_), n_sel)
        s = _bdot(q, kn_ref[0], _NT)
        q_t = lax.broadcasted_iota(jnp.int32, s.shape, 1)
        k_t = lax.broadcasted_iota(jnp.int32, s.shape, 2)
        s = jnp.where(k_t <= q_t, s, NEG_INF)
        m = jnp.max(s, axis=2, keepdims=True)
        p = jnp.exp(s - m)
        m_scr[...] = m
        l_scr[...] = jnp.sum(p, axis=2, keepdims=True)
        acc_scr[...] = _bdot(p, vn_ref[0], _NN)

    if n_sel > 0:
        sel = sel_scr[...]
        blk = lax.broadcasted_iota(jnp.int32, sel.shape, 2)
        scores = []
        for pg in range(n_pages):
            b = n * (n_pages // ppb) + pg // ppb
            chosen = jnp.sum(jnp.where(blk == b, sel, 0.0), axis=2, keepdims=True) > 0.0
            scores.append(jnp.where(chosen, _bdot(q, k_pages[pg][0, 0], _NN), NEG_INF))
        s = jnp.concatenate(scores, axis=2)
        m = m_scr[...]
        m_new = jnp.maximum(m, jnp.max(s, axis=2, keepdims=True))
        alpha = jnp.exp(m - m_new)
        p = jnp.exp(s - m_new)
        m_scr[...] = m_new
        l_scr[...] = alpha * l_scr[...] + jnp.sum(p, axis=2, keepdims=True)
        acc = alpha * acc_scr[...]
        for pg in range(n_pages):
            acc = acc + _bdot(p[:, :, pg * page_size:(pg + 1) * page_size], v_pages[pg][0, 0], _NT)
        acc_scr[...] = acc

    @pl.when(n == pl.num_programs(1) - 1)
    def _():
        o_ref[0] = acc_scr[...] / l_scr[...]


def moba_sample(cache_kt, cache_vt, layer, page_table, q_rows, k_rows, v_rows, kmean):
    _, _, n_heads, d, page_size = cache_kt.shape
    db, t, _ = q_rows.shape
    n_pages = page_table.shape[1]
    ppb = MOBA_BLOCK // page_size
    n_full = (n_pages * page_size) // MOBA_BLOCK
    assert n_full * ppb == n_pages, "past rows inside the current block are not supported"
    n_sel = min(MOBA_TOPK, n_full)
    tp = -(-t // SUBLANES) * SUBLANES
    hm = lambda x: jnp.pad(_to_heads(x), ((0, 0), (0, 0), (0, tp - t), (0, 0)))
    new_spec = pl.BlockSpec((1, n_heads, tp, d), lambda bi, n, pt: (bi, 0, 0, 0))

    pps = SAMPLE_BLOCKS_PER_STEP * ppb
    assert n_pages % pps == 0

    def page_spec(q):
        return pl.BlockSpec((1, 1, n_heads, d, page_size), lambda bi, n, pt: (layer, pt[bi, n * pps + q], 0, 0, 0))

    o = pl.pallas_call(
        functools.partial(_moba_sample_kernel, ppb=ppb, n_heads=n_heads, n_sel=n_sel),
        grid_spec=pltpu.PrefetchScalarGridSpec(
            num_scalar_prefetch=1,
            grid=(db, n_pages // pps),
            in_specs=[new_spec, new_spec, new_spec,
                      pl.BlockSpec((1, n_heads, n_full, d), lambda bi, n, pt: (bi, 0, 0, 0))]
            + [page_spec(q) for q in range(pps)] * 2,
            out_specs=new_spec,
            scratch_shapes=[
                pltpu.VMEM((n_heads, tp, n_full), F32),
                pltpu.VMEM((n_heads, tp, 1), F32),
                pltpu.VMEM((n_heads, tp, 1), F32),
                pltpu.VMEM((n_heads, tp, d), F32),
            ],
        ),
        out_shape=jax.ShapeDtypeStruct((db, n_heads, tp, d), F32),
        compiler_params=_cparams("parallel", "arbitrary"),
        name="moba_sample",
    )(page_table, hm(q_rows), hm(k_rows), hm(v_rows), kmean, *([cache_kt] * pps), *([cache_vt] * pps))
    return o[:, :, :t]


def _mix_out_kernel(y_ref, r_ref, k_ref, v_ref, g_ref, om_ref, x_ref, wo_ref, lnw_ref, lnb_ref, rk_ref, o_ref,
                    *, h_rwkv, h_moba):
    pieces = []
    for h in range(h_rwkv):
        y = y_ref[0, h]
        mean = jnp.mean(y, axis=-1, keepdims=True)
        yc = y - mean
        var = jnp.mean(yc * yc, axis=-1, keepdims=True)
        yn = yc * lax.rsqrt(var + GN_EPS) * lnw_ref[h:h + 1, :] + lnb_ref[h:h + 1, :]
        bonus = jnp.sum(r_ref[0, h] * k_ref[0, h] * rk_ref[h:h + 1, :], axis=-1, keepdims=True) * v_ref[0, h]
        pieces.append((yn + bonus) * g_ref[0, h])
    pieces += [om_ref[0, h] for h in range(h_moba)]
    o = jnp.concatenate(pieces, axis=1).astype(BF16)
    o_ref[0] = x_ref[0] + jnp.dot(o, wo_ref[...], preferred_element_type=F32)


def mix_out(y, r, k, v, g, o_moba, x, w_out, ln_w, ln_b, r_k, *, tt_target=512):
    b, h_rwkv, t, n = y.shape
    h_moba = o_moba.shape[1]
    d = x.shape[2]
    tt = _row_tile(t, tt_target)
    hm_r = pl.BlockSpec((1, h_rwkv, tt, n), lambda bi, i: (bi, 0, i, 0))
    hm_m = pl.BlockSpec((1, h_moba, tt, n), lambda bi, i: (bi, 0, i, 0))
    full = lambda shape: pl.BlockSpec(shape, lambda bi, i: (0,) * len(shape))
    params = [w_out, ln_w.reshape(h_rwkv, n), ln_b.reshape(h_rwkv, n), r_k.reshape(h_rwkv, n)]
    return pl.pallas_call(
        functools.partial(_mix_out_kernel, h_rwkv=h_rwkv, h_moba=h_moba),
        grid=(b, t // tt),
        in_specs=[hm_r] * 5 + [hm_m, pl.BlockSpec((1, tt, d), lambda bi, i: (bi, i, 0))] + [full(p.shape) for p in params],
        out_specs=pl.BlockSpec((1, tt, d), lambda bi, i: (bi, i, 0)),
        out_shape=jax.ShapeDtypeStruct((b, t, d), F32),
        compiler_params=_cparams("parallel", "parallel"),
        name="mix_out",
    )(y, r, k, v, g, o_moba, x, *params)


def _to_heads(x_rows):
    b, t, dm = x_rows.shape
    return x_rows.reshape(b, t, dm // HEAD_DIM, HEAD_DIM).transpose(0, 2, 1, 3)


def _mixer_front(y, lw, p_prev_rows, wkv_prev, cos, sin, *, head_major):
    b, t, d = y.shape
    flat = y.reshape(b * t, d)
    pr = norm_proj(flat, lw['mix_norm'], lw['w_in_r']).reshape(b, t, -1)
    pm = norm_proj(flat, lw['mix_norm'], lw['w_in_m']).reshape(b, t, -1)
    shift = rms_norm_rows(y[:, -1], lw['mix_norm'])
    r, log_w, k, v, kk, kka, g, *cum = rwkv_prep(pr, p_prev_rows, lw)
    if cum:
        y_wkv, wkv_new = wkv_chunk_scan(r, log_w, cum[0], k, v, kk, kka, wkv_prev)
    else:
        y_wkv, wkv_new = wkv_scan(r, log_w, k, v, kk, kka, wkv_prev)
    k_rows, *moba_in = rope_qkv(pm, cos, sin, head_major=head_major)
    d_m = k_rows.shape[2]
    v_rows = pm[:, :, 2 * d_m:]
    return dict(r=r, k=k, v=v, g=g, y=y_wkv, wkv=wkv_new, shift=shift, k_rows=k_rows, v_rows=v_rows, moba_in=moba_in)


@jax.jit
def kernel(x_prompt, x_sample, cache_k, cache_v, state_wkv, state_shift, page_table, ffn1_norm, ffn1_w_gate,
           ffn1_w_up, ffn1_w_down, mix_norm, w_in, w_out, rwkv_mu, rwkv_w0, rwkv_w2, rwkv_a0, rwkv_a2, rwkv_g2,
           rwkv_k_k, rwkv_k_a, rwkv_r_k, rwkv_ln_w, rwkv_ln_b, ffn2_norm, ffn2_w_gate, ffn2_w_up, ffn2_w_down,
           final_norm):
    b, t, d = x_prompt.shape
    db, ts, _ = x_sample.shape
    depth = w_in.shape[0]
    d_r = rwkv_w0.shape[1]
    h_rwkv = d_r // HEAD_DIM
    c_r = rwkv_mu.shape[1]
    d_m = (w_in.shape[2] - c_r) // 3
    h_moba = d_m // HEAD_DIM
    past_len = page_table.shape[1] * cache_k.shape[2]
    ck = cache_k.transpose(0, 1, 3, 4, 2)
    cv = cache_v.transpose(0, 1, 3, 4, 2)
    cos_p, sin_p = rope_tables(jnp.arange(t), h_moba)
    cos_s, sin_s = rope_tables(past_len + jnp.arange(ts), h_moba)

    yp, ys = x_prompt, x_sample
    outs = {n: [] for n in ('kp', 'vp', 'ks', 'vs', 'wp', 'ws', 'sp', 'ss')}
    for l in range(depth):
        lw = {'mix_norm': mix_norm[l], 'w_in_r': w_in[l, :, :c_r].astype(BF16), 'w_in_m': w_in[l, :, c_r:].astype(BF16),
              'mu': rwkv_mu[l], 'w0': rwkv_w0[l], 'w2': rwkv_w2[l], 'a0': rwkv_a0[l], 'a2': rwkv_a2[l],
              'g2': rwkv_g2[l], 'k_k': rwkv_k_k[l], 'k_a': rwkv_k_a[l]}
        wo = w_out[l].astype(BF16)
        f1 = (ffn1_norm[l], ffn1_w_gate[l].astype(BF16), ffn1_w_up[l].astype(BF16), ffn1_w_down[l].astype(BF16))
        f2 = (ffn2_norm[l], ffn2_w_gate[l].astype(BF16), ffn2_w_up[l].astype(BF16), ffn2_w_down[l].astype(BF16))

        yp = ffn_half(yp.reshape(b * t, d), *f1).reshape(b, t, d)
        ys = ffn_half(ys.reshape(db * ts, d), *f1).reshape(db, ts, d)

        fp = _mixer_front(yp, lw, jnp.zeros((b, c_r), F32), jnp.zeros((b, h_rwkv, HEAD_DIM, HEAD_DIM), F32),
                          cos_p, sin_p, head_major=True)
        p_prev_s = norm_proj(state_shift[l], lw['mix_norm'], lw['w_in_r'], normalize=False)
        fs = _mixer_front(ys, lw, p_prev_s, state_wkv[l], cos_s, sin_s, head_major=False)

        kmean_p = _to_heads(block_means(fp['k_rows']))
        om_p = moba_prompt(*fp['moba_in'], kmean_p)
        kmean_s = sample_block_means(ck, l, page_table).transpose(0, 2, 1, 3)
        om_s = moba_sample(ck, cv, l, page_table, fs['moba_in'][0], fs['k_rows'], fs['v_rows'], kmean_s)

        mix = lambda f, om, y: mix_out(f['y'], f['r'], f['k'], f['v'], f['g'], om, y, wo,
                                       rwkv_ln_w[l], rwkv_ln_b[l], rwkv_r_k[l])
        yp = mix(fp, om_p, yp)
        ys = mix(fs, om_s, ys)

        yp = ffn_half(yp.reshape(b * t, d), *f2).reshape(b, t, d)
        ys = ffn_half(ys.reshape(db * ts, d), *f2).reshape(db, ts, d)

        outs['kp'].append(fp['k_rows'].reshape(b, t, h_moba, HEAD_DIM))
        outs['vp'].append(fp['v_rows'].reshape(b, t, h_moba, HEAD_DIM))
        outs['ks'].append(fs['k_rows'].reshape(db, ts, h_moba, HEAD_DIM))
        outs['vs'].append(fs['v_rows'].reshape(db, ts, h_moba, HEAD_DIM))
        outs['wp'].append(fp['wkv'])
        outs['ws'].append(fs['wkv'])
        outs['sp'].append(fp['shift'])
        outs['ss'].append(fs['shift'])

    y_prompt = rms_norm_rows(yp.reshape(b * t, d), final_norm).reshape(b, t, d)
    y_sample = rms_norm_rows(ys.reshape(db * ts, d), final_norm).reshape(db, ts, d)
    st = lambda n: jnp.stack(outs[n])
    return (y_prompt, y_sample, st('kp'), st('vp'), st('ks'), st('vs'), st('wp'), st('ws'), st('sp'), st('ss'))
```
